```python
import math
import jax, jax.numpy as jnp
from jax import lax
import numpy as np

D_MODEL = 2048
BATCH = 8
SEQ = 2048
DEPTH = 2
DEC_BATCH = 8
DEC_SEQ = 4096
PAST_LEN = 128

HEAD_DIM = 128
GRID_W = 64
Q_BLOCK = 128
EPS = 1e-6
NEG_INF = -1e30

A_HEADS = D_MODEL // (4 * HEAD_DIM)
B_HEADS = D_MODEL // (2 * HEAD_DIM)
NA_KH_MAX = 8
NA_KW = 16
C_PATTERNS = ((128, 1), (512, 4), (2048, 16))
C_HEADS_PER_GROUP = D_MODEL // (4 * HEAD_DIM)
C_HEADS = len(C_PATTERNS) * C_HEADS_PER_GROUP
D_HEADS = 3 * D_MODEL // (4 * HEAD_DIM)
D_KV_HEADS = D_HEADS // 3
ROPE_THETA = 10000.0
ROPE_ROW_DIMS = HEAD_DIM // 2
ROPE_COL_DIMS = HEAD_DIM - ROPE_ROW_DIMS
FFN_HIDDEN = ((8 * D_MODEL + 3 * 256 - 1) // (3 * 256)) * 256

A_W = A_HEADS * 2 * HEAD_DIM
B_W = B_HEADS * HEAD_DIM
EV_IN = 3 * A_W + 3 * B_W
EV_OUT = A_W + B_W
C_W = C_HEADS * HEAD_DIM
DQ_W = D_HEADS * HEAD_DIM
DKV_W = D_KV_HEADS * HEAD_DIM
OD_IN = 3 * C_W + DQ_W + 2 * DKV_W
OD_OUT = C_HEADS_PER_GROUP * HEAD_DIM + DQ_W
N_EVEN = (DEPTH + 1) // 2
N_ODD = DEPTH // 2

kernel_name = "hybrid_diff_na_dilated_axial_encoder"


def rms_norm(x, g):
    xf = x.astype(jnp.float32)
    y = xf * lax.rsqrt(jnp.mean(xf * xf, axis=-1, keepdims=True) + EPS)
    return (y * g.astype(jnp.float32)).astype(x.dtype)


def alibi_slopes(n):
    return jnp.asarray([2.0 ** (-8.0 * (i + 1) / n) for i in range(n)], dtype=jnp.float32)


def lambda_init_fn(layer_idx):
    return 0.8 - 0.6 * math.exp(-0.3 * layer_idx)


def diff_attention(q, k, v, lam, slopes):
    B, H, S, _, hd = q.shape
    nblk = S // Q_BLOCK
    scale = hd ** -0.5
    kpos = jnp.arange(S)
    qblocks = q.reshape(B, H, nblk, Q_BLOCK, 2, hd).transpose(2, 0, 1, 3, 4, 5)

    def block(args):
        qi, bi = args
        s = jnp.einsum("bhqcd,bhkcd->bhcqk", qi, k, preferred_element_type=jnp.float32) * scale
        qpos = bi * Q_BLOCK + jnp.arange(Q_BLOCK)
        dist = jnp.abs(qpos[:, None] - kpos[None, :]).astype(jnp.float32)
        s = s - (slopes[:, None, None] * dist)[None, :, None]
        p = jax.nn.softmax(s, axis=-1)
        a = p[:, :, 0] - lam * p[:, :, 1]
        return jnp.einsum("bhqk,bhkd->bhqd", a.astype(v.dtype), v)

    o = lax.map(block, (qblocks, jnp.arange(nblk)))
    return o.transpose(1, 2, 0, 3, 4).reshape(B, H, S, 2 * hd)


def neighbourhood_attention(q, k, v, rpb):
    B, H, S, hd = q.shape
    rows = S // GRID_W
    kh = min(NA_KH_MAX, rows)
    scale = hd ** -0.5
    qg = q.reshape(B, H, rows, GRID_W, hd)
    kg = k.reshape(B, H, rows, GRID_W, hd)
    vg = v.reshape(B, H, rows, GRID_W, hd)
    c = jnp.arange(GRID_W)
    c0 = jnp.clip(c - NA_KW // 2, 0, GRID_W - NA_KW)
    col_mask = (c[None, :] >= c0[:, None]) & (c[None, :] < c0[:, None] + NA_KW)
    dc = jnp.clip(c[None, :] - c[:, None], 1 - NA_KW, NA_KW - 1) + NA_KW - 1
    rpb_cols = jnp.take(rpb, dc, axis=2)

    def row_block(r):
        r0 = jnp.clip(r - kh // 2, 0, rows - kh)
        qr = lax.dynamic_index_in_dim(qg, r, axis=2, keepdims=False)
        kr = lax.dynamic_slice_in_dim(kg, r0, kh, axis=2)
        vr = lax.dynamic_slice_in_dim(vg, r0, kh, axis=2)
        s = jnp.einsum("bhqd,bhiwd->bhqiw", qr, kr, preferred_element_type=jnp.float32) * scale
        dr = r0 + jnp.arange(kh) - r + NA_KH_MAX - 1
        bias = jnp.take(rpb_cols, dr, axis=1).transpose(0, 2, 1, 3)
        s = jnp.where(col_mask[:, None, :], s + bias[None], NEG_INF)
        p = jax.nn.softmax(s.reshape(B, H, GRID_W, kh * GRID_W), axis=-1).reshape(s.shape)
        return jnp.einsum("bhqiw,bhiwd->bhqd", p.astype(v.dtype), vr)

    o = lax.map(row_block, jnp.arange(rows))
    return o.transpose(1, 2, 0, 3, 4).reshape(B, H, S, hd)


def dilated_group(q, k, v, dil, radius, slopes):
    B, H, S, hd = q.shape
    L = S // dil
    scale = hd ** -0.5

    def split(a):
        return a.reshape(B, H, L, dil, hd).transpose(0, 1, 3, 2, 4)

    qs, ks, vs = split(q), split(k), split(v)
    qb = min(Q_BLOCK, L)
    nblk = -(-L // qb)
    Lp = nblk * qb
    qs = jnp.pad(qs, ((0, 0), (0, 0), (0, 0), (0, Lp - L), (0, 0)))
    kpad = ((0, 0), (0, 0), (0, 0), (radius, radius + Lp - L), (0, 0))
    kp, vp = jnp.pad(ks, kpad), jnp.pad(vs, kpad)
    band = jnp.arange(nblk)[:, None] * qb + jnp.arange(qb + 2 * radius)[None, :]
    kb = jnp.take(kp, band, axis=3)
    vb = jnp.take(vp, band, axis=3)
    qbk = qs.reshape(B, H, dil, nblk, qb, hd)
    s = jnp.einsum("bhrnqd,bhrnkd->bhrnqk", qbk, kb, preferred_element_type=jnp.float32) * scale
    qm = jnp.arange(nblk)[:, None] * qb + jnp.arange(qb)[None, :]
    km = band - radius
    rel = km[:, None, :] - qm[:, :, None]
    valid = (jnp.abs(rel) <= radius) & (km[:, None, :] >= 0) & (km[:, None, :] < L)
    dist = (dil * jnp.abs(rel)).astype(jnp.float32)
    bias = -slopes[:, None, None, None] * dist
    s = jnp.where(valid, s + bias[None, :, None], NEG_INF)
    lse = jax.nn.logsumexp(s, axis=-1)
    p = jnp.exp(s - lse[..., None])
    o = jnp.einsum("bhrnqk,bhrnkd->bhrnqd", p.astype(v.dtype), vb)
    o = o.reshape(B, H, dil, Lp, hd)[:, :, :, :L].transpose(0, 1, 3, 2, 4).reshape(B, H, S, hd)
    lse = lse.reshape(B, H, dil, Lp)[:, :, :, :L].transpose(0, 1, 3, 2).reshape(B, H, S)
    return o, lse


def axial_rope(S):
    t = jnp.arange(S)
    row = (t // GRID_W).astype(jnp.float32)
    col = (t % GRID_W).astype(jnp.float32)
    f_row = ROPE_THETA ** (-jnp.arange(0, ROPE_ROW_DIMS, 2, dtype=jnp.float32) / ROPE_ROW_DIMS)
    f_col = ROPE_THETA ** (-jnp.arange(0, ROPE_COL_DIMS, 2, dtype=jnp.float32) / ROPE_COL_DIMS)
    ang = jnp.concatenate([row[:, None] * f_row[None, :], col[:, None] * f_col[None, :]], axis=-1)
    return jnp.cos(ang), jnp.sin(ang)


def apply_rope(x, cos, sin):
    xf = x.astype(jnp.float32).reshape(x.shape[:-1] + (x.shape[-1] // 2, 2))
    x0, x1 = xf[..., 0], xf[..., 1]
    y = jnp.stack([x0 * cos - x1 * sin, x0 * sin + x1 * cos], axis=-1)
    return y.reshape(x.shape).astype(x.dtype)


def gqa_attention(q, k, v):
    B, Hq, S, hd = q.shape
    Hkv = k.shape[1]
    G = Hq // Hkv
    nblk = S // Q_BLOCK
    scale = hd ** -0.5
    qblocks = q.reshape(B, Hkv, G, nblk, Q_BLOCK, hd).transpose(3, 0, 1, 2, 4, 5)

    def block(qi):
        s = jnp.einsum("bkgqd,bksd->bkgqs", qi, k, preferred_element_type=jnp.float32) * scale
        p = jax.nn.softmax(s, axis=-1)
        return jnp.einsum("bkgqs,bksd->bkgqd", p.astype(v.dtype), v)

    o = lax.map(block, qblocks)
    return o.transpose(1, 2, 3, 0, 4, 5).reshape(B, Hq, S, hd)


def heads(a, n):
    B, S, _ = a.shape
    return a.reshape(B, S, n, -1).transpose(0, 2, 1, 3)


def even_mixer(h, w_in, lam_vec, subln_g, rpb, w_out, lambda_init):
    B, S, _ = h.shape
    proj = h @ w_in
    cuts = np.cumsum([A_W, A_W, A_W, B_W, B_W]).tolist()
    qa, ka, va, qn, kn, vn = jnp.split(proj, cuts, axis=-1)
    qa = qa.reshape(B, S, A_HEADS, 2, HEAD_DIM).transpose(0, 2, 1, 3, 4)
    ka = ka.reshape(B, S, A_HEADS, 2, HEAD_DIM).transpose(0, 2, 1, 3, 4)
    va = heads(va, A_HEADS)
    lv = lam_vec.astype(jnp.float32)
    lam = jnp.exp(jnp.sum(lv[0] * lv[1])) - jnp.exp(jnp.sum(lv[2] * lv[3])) + lambda_init
    oa = diff_attention(qa, ka, va, lam, alibi_slopes(A_HEADS))
    oa = rms_norm(oa, subln_g) * (1.0 - lambda_init)
    oa = oa.transpose(0, 2, 1, 3).reshape(B, S, A_W)
    ob = neighbourhood_attention(heads(qn, B_HEADS), heads(kn, B_HEADS), heads(vn, B_HEADS), rpb)
    ob = ob.transpose(0, 2, 1, 3).reshape(B, S, B_W)
    return jnp.concatenate([oa, ob], axis=-1) @ w_out


def odd_mixer(h, w_in, qk_norm_g, w_out):
    B, S, _ = h.shape
    proj = h @ w_in
    cuts = np.cumsum([C_W, C_W, C_W, DQ_W, DKV_W]).tolist()
    qc, kc, vc, qd, kd, vd = jnp.split(proj, cuts, axis=-1)
    ng = len(C_PATTERNS)
    qc = qc.reshape(B, S, ng, C_HEADS_PER_GROUP, HEAD_DIM)
    kc = kc.reshape(B, S, ng, C_HEADS_PER_GROUP, HEAD_DIM)
    vc = vc.reshape(B, S, ng, C_HEADS_PER_GROUP, HEAD_DIM)
    slopes = alibi_slopes(C_HEADS).reshape(ng, C_HEADS_PER_GROUP)
    outs, lses = [], []
    for g, (win, dil) in enumerate(C_PATTERNS):
        o, l = dilated_group(qc[:, :, g].transpose(0, 2, 1, 3), kc[:, :, g].transpose(0, 2, 1, 3),
                             vc[:, :, g].transpose(0, 2, 1, 3), dil, win // (2 * dil), slopes[g])
        outs.append(o)
        lses.append(l)
    alpha = jax.nn.softmax(jnp.stack(lses, axis=0), axis=0)
    oc = jnp.sum(alpha[..., None] * jnp.stack(outs, axis=0).astype(jnp.float32), axis=0).astype(h.dtype)
    oc = oc.transpose(0, 2, 1, 3).reshape(B, S, C_HEADS_PER_GROUP * HEAD_DIM)
    cos, sin = axial_rope(S)
    qd = apply_rope(rms_norm(heads(qd, D_HEADS), qk_norm_g[0]), cos, sin)
    kd = apply_rope(rms_norm(heads(kd, D_KV_HEADS), qk_norm_g[1]), cos, sin)
    od = gqa_attention(qd, kd, heads(vd, D_KV_HEADS))
    od = od.transpose(0, 2, 1, 3).reshape(B, S, DQ_W)
    return jnp.concatenate([oc, od], axis=-1) @ w_out


def swiglu(h, w_gate, w_up, w_down):
    return (jax.nn.silu(h @ w_gate) * (h @ w_up)) @ w_down


def trunk(x, attn_norm_g, ev_w_in, ev_lambda, ev_subln_g, ev_rpb, ev_w_out,
          od_w_in, od_qk_norm_g, od_w_out, ffn_norm_g, ffn_w_gate, ffn_w_up, ffn_w_down,
          final_norm_g):
    h = x
    for i in range(DEPTH):
        hn = rms_norm(h, attn_norm_g[i])
        j = i // 2
        if i % 2 == 0:
            h = h + even_mixer(hn, ev_w_in[j], ev_lambda[j], ev_subln_g[j], ev_rpb[j],
                               ev_w_out[j], lambda_init_fn(i))
        else:
            h = h + odd_mixer(hn, od_w_in[j], od_qk_norm_g[j], od_w_out[j])
        h = h + swiglu(rms_norm(h, ffn_norm_g[i]), ffn_w_gate[i], ffn_w_up[i], ffn_w_down[i])
    return rms_norm(h, final_norm_g)


def setup_inputs(seed: int = 0) -> dict:
    key = jax.random.key(seed)
    ks = jax.random.split(key, 17)
    f32 = jnp.float32

    def nrm(k, shape, scale):
        return jax.random.normal(k, shape, f32) * scale

    D = D_MODEL
    return {
        "x_prompt": nrm(ks[0], (BATCH, SEQ, D), 1.0),
        "x_sample": nrm(ks[1], (DEC_BATCH, DEC_SEQ, D), 1.0),
        "attn_norm_g": 1.0 + nrm(ks[2], (DEPTH, D), 0.02),
        "ev_w_in": nrm(ks[3], (N_EVEN, D, EV_IN), D ** -0.5),
        "ev_lambda": nrm(ks[4], (N_EVEN, 4, HEAD_DIM), 0.1),
        "ev_subln_g": 1.0 + nrm(ks[5], (N_EVEN, 2 * HEAD_DIM), 0.02),
        "ev_rpb": nrm(ks[6], (N_EVEN, B_HEADS, 2 * NA_KH_MAX - 1, 2 * NA_KW - 1), 0.1),
        "ev_w_out": nrm(ks[7], (N_EVEN, EV_OUT, D), EV_OUT ** -0.5),
        "od_w_in": nrm(ks[8], (N_ODD, D, OD_IN), D ** -0.5),
        "od_qk_norm_g": 1.0 + nrm(ks[9], (N_ODD, 2, HEAD_DIM), 0.02),
        "od_w_out": nrm(ks[10], (N_ODD, OD_OUT, D), OD_OUT ** -0.5),
        "ffn_norm_g": 1.0 + nrm(ks[11], (DEPTH, D), 0.02),
        "ffn_w_gate": nrm(ks[12], (DEPTH, D, FFN_HIDDEN), D ** -0.5),
        "ffn_w_up": nrm(ks[13], (DEPTH, D, FFN_HIDDEN), D ** -0.5),
        "ffn_w_down": nrm(ks[14], (DEPTH, FFN_HIDDEN, D), FFN_HIDDEN ** -0.5),
        "final_norm_g": 1.0 + nrm(ks[15], (D,), 0.02),
    }


def reference(x_prompt, x_sample, attn_norm_g, ev_w_in, ev_lambda, ev_subln_g, ev_rpb, ev_w_out,
              od_w_in, od_qk_norm_g, od_w_out, ffn_norm_g, ffn_w_gate, ffn_w_up, ffn_w_down,
              final_norm_g):
    y_prompt = trunk(x_prompt, attn_norm_g, ev_w_in, ev_lambda, ev_subln_g, ev_rpb, ev_w_out,
                     od_w_in, od_qk_norm_g, od_w_out, ffn_norm_g, ffn_w_gate, ffn_w_up, ffn_w_down,
                     final_norm_g)
    y_sample = trunk(x_sample, attn_norm_g, ev_w_in, ev_lambda, ev_subln_g, ev_rpb, ev_w_out,
                     od_w_in, od_qk_norm_g, od_w_out, ffn_norm_g, ffn_w_gate, ffn_w_up, ffn_w_down,
                     final_norm_g)
    return (y_prompt, y_sample)
```

```python
import functools
import math

import jax
import jax.numpy as jnp
from jax import lax
from jax.experimental import pallas as pl
from jax.experimental.pallas import tpu as pltpu

F32 = jnp.float32
BF16 = jnp.bfloat16

D_MODEL = 2048
DEPTH = 2
HEAD_DIM = 128
GRID_W = 64
EPS = 1e-6
NEG_INF = -1e30
SCALE = HEAD_DIM ** -0.5

A_HEADS = 4
B_HEADS = 8
NA_KH = 8
NA_KW = 16
C_PATTERNS = ((128, 1), (512, 4), (2048, 16))
C_GROUP_HEADS = 4
C_HEADS = 12
D_HEADS = 12
D_KV_HEADS = 4
D_GROUP = D_HEADS // D_KV_HEADS
ROPE_THETA = 10000.0
FFN_HIDDEN = 5632

A_W = A_HEADS * 2 * HEAD_DIM
B_W = B_HEADS * HEAD_DIM
EV_IN = 3 * A_W + 3 * B_W
C_W = C_HEADS * HEAD_DIM
DQ_W = D_HEADS * HEAD_DIM
DKV_W = D_KV_HEADS * HEAD_DIM
OD_IN = 3 * C_W + DQ_W + 2 * DKV_W
OC_W = C_GROUP_HEADS * HEAD_DIM

VMEM_LIMIT_BYTES = 56 * 1024 * 1024
LANES = 128

PROJ_TM = 1024
PROJ_TN = 1024
FFN_TM = 1024
FFN_TH = 512
NORM_ROW_CHUNK = 256
ATTN_TQ = 256
DIL_TL = 128
DIL_RADIUS = 64
NA_ROWS_PER_STEP = 4


def _params(*sem):
    return pltpu.CompilerParams(dimension_semantics=sem, vmem_limit_bytes=VMEM_LIMIT_BYTES)


def _nt_dot(a, b):
    return lax.dot_general(a, b, (((1,), (1,)), ((), ())), preferred_element_type=F32)


def _rms(x, g):
    ms = jnp.mean(x * x, axis=-1, keepdims=True)
    return (x * lax.rsqrt(ms + EPS)) * g


def _norm_rows_to(x_ref, g_ref, xn_ref, rows):
    g = g_ref[...]

    def body(c, carry):
        r = pl.multiple_of(c * NORM_ROW_CHUNK, NORM_ROW_CHUNK)
        xn_ref[pl.ds(r, NORM_ROW_CHUNK), :] = _rms(x_ref[pl.ds(r, NORM_ROW_CHUNK), :], g).astype(BF16)
        return carry

    lax.fori_loop(0, rows // NORM_ROW_CHUNK, body, 0)


def _norm_proj_kernel(x_ref, g_ref, w_ref, o_ref, xn_ref):
    @pl.when(pl.program_id(1) == 0)
    def _():
        _norm_rows_to(x_ref, g_ref, xn_ref, x_ref.shape[0])

    o_ref[...] = jnp.dot(xn_ref[...], w_ref[...], preferred_element_type=F32).astype(o_ref.dtype)


def norm_proj(x, g, w):
    T, D = x.shape
    N = w.shape[1]
    tm, tn = PROJ_TM, PROJ_TN
    return pl.pallas_call(
        _norm_proj_kernel,
        grid=(T // tm, N // tn),
        in_specs=[
            pl.BlockSpec((tm, D), lambda i, j: (i, 0)),
            pl.BlockSpec((1, D), lambda i, j: (0, 0)),
            pl.BlockSpec((D, tn), lambda i, j: (0, j)),
        ],
        out_specs=pl.BlockSpec((tm, tn), lambda i, j: (i, j)),
        out_shape=jax.ShapeDtypeStruct((T, N), BF16),
        scratch_shapes=[pltpu.VMEM((tm, D), BF16)],
        compiler_params=_params("parallel", "arbitrary"),
        name="norm_proj",
    )(x, g.reshape(1, D), w)


def _qk_norm_rope(x, g, cos, sin_signed):
    y = _rms(x, g)
    lane = lax.broadcasted_iota(jnp.int32, y.shape, 1)
    nxt = pltpu.roll(y, LANES - 1, 1)
    prv = pltpu.roll(y, 1, 1)
    partner = jnp.where((lane & 1) == 0, nxt, prv)
    return y * cos + partner * sin_signed


def _od_head_kinds(tn):
    slots = tn // LANES
    q_lo, q_hi = 3 * C_W // LANES, (3 * C_W + DQ_W) // LANES
    k_hi = q_hi + DKV_W // LANES
    table = {}
    for j in range(OD_IN // tn):
        kinds = []
        for c in range(slots):
            s = j * slots + c
            kinds.append(0 if q_lo <= s < q_hi else (1 if q_hi <= s < k_hi else None))
        if any(k is not None for k in kinds):
            table[j] = kinds
    return table


def _od_proj_kernel(x_ref, g_ref, w_ref, cos_ref, sin_ref, qkg_ref, o_ref, xn_ref, *, kinds_by_tile):
    j = pl.program_id(1)

    @pl.when(j == 0)
    def _():
        _norm_rows_to(x_ref, g_ref, xn_ref, x_ref.shape[0])

    acc = jnp.dot(xn_ref[...], w_ref[...], preferred_element_type=F32)

    plain = None
    for jj, kinds in kinds_by_tile.items():
        hit = j == jj
        plain = jnp.logical_not(hit) if plain is None else jnp.logical_and(plain, jnp.logical_not(hit))

        @pl.when(hit)
        def _(kinds=kinds):
            cos = cos_ref[...]
            sin = sin_ref[...]
            for c, kind in enumerate(kinds):
                blk = acc[:, c * LANES:(c + 1) * LANES]
                if kind is not None:
                    blk = _qk_norm_rope(blk, qkg_ref[kind:kind + 1, :], cos, sin)
                o_ref[:, c * LANES:(c + 1) * LANES] = blk.astype(o_ref.dtype)

    @pl.when(plain)
    def _():
        o_ref[...] = acc.astype(o_ref.dtype)


def od_norm_proj(x, g, w, cos, sin_signed, qk_g, S):
    T, D = x.shape
    N = w.shape[1]
    tm, tn = PROJ_TM, PROJ_TN
    s_tiles = S // tm
    kernel = functools.partial(_od_proj_kernel, kinds_by_tile=_od_head_kinds(tn))
    return pl.pallas_call(
        kernel,
        grid=(T // tm, N // tn),
        in_specs=[
            pl.BlockSpec((tm, D), lambda i, j: (i, 0)),
            pl.BlockSpec((1, D), lambda i, j: (0, 0)),
            pl.BlockSpec((D, tn), lambda i, j: (0, j)),
            pl.BlockSpec((tm, HEAD_DIM), lambda i, j: (i % s_tiles, 0)),
            pl.BlockSpec((tm, HEAD_DIM), lambda i, j: (i % s_tiles, 0)),
            pl.BlockSpec((2, HEAD_DIM), lambda i, j: (0, 0)),
        ],
        out_specs=pl.BlockSpec((tm, tn), lambda i, j: (i, j)),
        out_shape=jax.ShapeDtypeStruct((T, N), BF16),
        scratch_shapes=[pltpu.VMEM((tm, D), BF16)],
        compiler_params=_params("parallel", "arbitrary"),
        name="od_norm_proj",
    )(x, g.reshape(1, D), w, cos, sin_signed, qk_g)


def _diff_attn_kernel(slopes_ref, lam_ref, g_ref, q_ref, k_ref, v_ref, o_ref, bias_ref, *, S, tq, lambda_init):
    h = pl.program_id(0)
    b = pl.program_id(1)
    i = pl.program_id(2)
    width = 2 * S - tq
    chunk = tq

    @pl.when(jnp.logical_and(b == 0, i == 0))
    def _():
        slope = slopes_ref[h]
        row = lax.broadcasted_iota(jnp.int32, (tq, chunk), 0)
        col = lax.broadcasted_iota(jnp.int32, (tq, chunk), 1)
        base = row + (S - tq) - col

        def body(c, carry):
            off = pl.multiple_of(c * chunk, chunk)
            dist = jnp.abs(base - off).astype(F32)
            bias_ref[:, pl.ds(off, chunk)] = -(slope * dist)
            return carry

        lax.fori_loop(0, width // chunk, body, 0)

    start = pl.multiple_of(S - tq - i * tq, LANES)
    lv = lam_ref[...]
    lam = (jnp.exp(jnp.sum(lv[0:1] * lv[1:2], axis=-1, keepdims=True))
           - jnp.exp(jnp.sum(lv[2:3] * lv[3:4], axis=-1, keepdims=True)) + lambda_init)

    def component(c):
        q = q_ref[:, c * HEAD_DIM:(c + 1) * HEAD_DIM]
        k = k_ref[:, c * HEAD_DIM:(c + 1) * HEAD_DIM]
        s = _nt_dot(q, k) * SCALE + bias_ref[:, pl.ds(start, S)]
        m = jnp.max(s, axis=-1, keepdims=True)
        p = jnp.exp(s - m)
        l = jnp.sum(p, axis=-1, keepdims=True)
        o = jnp.dot(p.astype(BF16), v_ref[...], preferred_element_type=F32)
        return o / l

    o = component(0) - lam * component(1)
    o = _rms(o, g_ref[...]) * (1.0 - lambda_init)
    o_ref[...] = o.astype(o_ref.dtype)


def diff_attention(proj, lam_vec, subln_g, B, S, lambda_init):
    tq = ATTN_TQ
    dv = 2 * HEAD_DIM
    slopes = jnp.asarray([2.0 ** (-8.0 * (i + 1) / A_HEADS) for i in range(A_HEADS)], F32)
    kernel = functools.partial(_diff_attn_kernel, S=S, tq=tq, lambda_init=lambda_init)
    return pl.pallas_call(
        kernel,
        grid=(A_HEADS, B, S // tq),
        in_specs=[
            pl.BlockSpec(memory_space=pltpu.SMEM),
            pl.BlockSpec((4, HEAD_DIM), lambda h, b, i: (0, 0)),
            pl.BlockSpec((1, dv), lambda h, b, i: (0, 0)),
            pl.BlockSpec((None, tq, dv), lambda h, b, i: (b, i, h)),
            pl.BlockSpec((None, S, dv), lambda h, b, i: (b, 0, A_HEADS + h)),
            pl.BlockSpec((None, S, dv), lambda h, b, i: (b, 0, 2 * A_HEADS + h)),
        ],
        out_specs=pl.BlockSpec((None, tq, dv), lambda h, b, i: (b, i, h)),
        out_shape=jax.ShapeDtypeStruct((B, S, A_W), BF16),
        scratch_shapes=[pltpu.VMEM((tq, 2 * S - tq), F32)],
        compiler_params=_params("arbitrary", "arbitrary", "arbitrary"),
        name="diff_attention",
    )(slopes, lam_vec.astype(F32), subln_g.reshape(1, dv), proj, proj, proj)


RPB_ROWS = 2 * NA_KH - 1
RPB_COLS = 2 * NA_KW - 1


def _nbr_attn_kernel(rpb_ref, q_ref, k_ref, v_ref, o_ref, bias_ref, *, S):
    h = pl.program_id(0)
    b = pl.program_id(1)
    rows = S // GRID_W
    win = NA_KH * GRID_W

    @pl.when(b == 0)
    def _():
        c = lax.broadcasted_iota(jnp.int32, (GRID_W, LANES), 0)
        lane = lax.broadcasted_iota(jnp.int32, (GRID_W, LANES), 1)
        w = lane & (GRID_W - 1)
        upper = lane >= GRID_W
        c0 = jnp.clip(c - NA_KW // 2, 0, GRID_W - NA_KW)
        valid = jnp.logical_and(w >= c0, w < c0 + NA_KW)
        dc = jnp.clip(w - c, 1 - NA_KW, NA_KW - 1) + NA_KW - 1
        base = h * (RPB_ROWS * RPB_COLS)
        for dr in range(RPB_ROWS - 1):
            def body(d, acc, dr=dr):
                lo = rpb_ref[base + dr * RPB_COLS + d]
                hi = rpb_ref[base + (dr + 1) * RPB_COLS + d]
                return jnp.where(dc == d, jnp.where(upper, hi, lo), acc)

            acc = lax.fori_loop(0, RPB_COLS, body, jnp.zeros((GRID_W, LANES), F32))
            bias_ref[dr] = jnp.where(valid, acc, NEG_INF)

    def step(gi, carry):
        for a in range(NA_ROWS_PER_STEP):
            r = gi * NA_ROWS_PER_STEP + a
            r0 = jnp.clip(r - NA_KH // 2, 0, rows - NA_KH)
            dr0 = r0 - r + NA_KH - 1
            qs = pl.multiple_of(r * GRID_W, GRID_W)
            ks = pl.multiple_of(r0 * GRID_W, GRID_W)
            q = q_ref[pl.ds(qs, GRID_W), :]
            k = k_ref[pl.ds(ks, win), :]
            v = v_ref[pl.ds(ks, win), :]
            bias = jnp.concatenate([bias_ref[dr0 + 2 * t] for t in range(NA_KH // 2)], axis=1)
            s = _nt_dot(q, k) * SCALE + bias
            m = jnp.max(s, axis=-1, keepdims=True)
            p = jnp.exp(s - m)
            l = jnp.sum(p, axis=-1, keepdims=True)
            o = jnp.dot(p.astype(BF16), v, preferred_element_type=F32) / l
            o_ref[pl.ds(qs, GRID_W), :] = o.astype(o_ref.dtype)
        return carry

    lax.fori_loop(0, rows // NA_ROWS_PER_STEP, step, 0)


def neighbourhood_attention(proj, rpb, B, S):
    q0 = 3 * A_W // HEAD_DIM
    kernel = functools.partial(_nbr_attn_kernel, S=S)
    return pl.pallas_call(
        kernel,
        grid=(B_HEADS, B),
        in_specs=[
            pl.BlockSpec(memory_space=pltpu.SMEM),
            pl.BlockSpec((None, S, HEAD_DIM), lambda h, b: (b, 0, q0 + h)),
            pl.BlockSpec((None, S, HEAD_DIM), lambda h, b: (b, 0, q0 + B_HEADS + h)),
            pl.BlockSpec((None, S, HEAD_DIM), lambda h, b: (b, 0, q0 + 2 * B_HEADS + h)),
        ],
        out_specs=pl.BlockSpec((None, S, HEAD_DIM), lambda h, b: (b, 0, h)),
        out_shape=jax.ShapeDtypeStruct((B, S, B_W), BF16),
        scratch_shapes=[pltpu.VMEM((RPB_ROWS - 1, GRID_W, LANES), F32)],
        compiler_params=_params("arbitrary", "arbitrary"),
        name="neighbourhood_attention",
    )(rpb.astype(F32).reshape(-1), proj, proj, proj)


def _dilated_kernel(q_ref, k_ref, v_ref, o_ref, lse_ref, *, L, tl, kw, dil, slopes):
    l0 = pl.program_id(2) * tl
    ks = pl.multiple_of(jnp.clip(l0 - DIL_RADIUS, 0, L - kw), DIL_RADIUS)
    mq = l0 + lax.broadcasted_iota(jnp.int32, (tl, kw), 0)
    mk = ks + lax.broadcasted_iota(jnp.int32, (tl, kw), 1)
    steps = jnp.abs(mk - mq)
    valid = steps <= DIL_RADIUS
    dist = (dil * steps).astype(F32)
    for hh in range(C_GROUP_HEADS):
        cols = slice(hh * HEAD_DIM, (hh + 1) * HEAD_DIM)
        q = q_ref[:, cols]
        k = k_ref[pl.ds(ks, kw), cols]
        v = v_ref[pl.ds(ks, kw), cols]
        s = _nt_dot(q, k) * SCALE
        s = jnp.where(valid, s - slopes[hh] * dist, NEG_INF)
        m = jnp.max(s, axis=-1, keepdims=True)
        p = jnp.exp(s - m)
        l = jnp.sum(p, axis=-1, keepdims=True)
        o_ref[:, cols] = jnp.dot(p.astype(BF16), v, preferred_element_type=F32) / l
        lse_ref[:, cols] = jnp.broadcast_to(m + jnp.log(l), (tl, HEAD_DIM))


def dilated_group(proj, B, S, group):
    win, dil = C_PATTERNS[group]
    assert win // (2 * dil) == DIL_RADIUS
    L = S // dil
    tl = min(DIL_TL, L)
    kw = min(L, tl + 2 * DIL_RADIUS)
    slopes = tuple(2.0 ** (-8.0 * (group * C_GROUP_HEADS + hh + 1) / C_HEADS) for hh in range(C_GROUP_HEADS))
    blocks_per_pos = OD_IN // OC_W
    q_blk, k_blk, v_blk = group, C_W // OC_W + group, 2 * C_W // OC_W + group
    view = proj.reshape(B, L, dil * OD_IN)
    kernel = functools.partial(_dilated_kernel, L=L, tl=tl, kw=kw, dil=dil, slopes=slopes)
    o, lse = pl.pallas_call(
        kernel,
        grid=(B, dil, L // tl),
        in_specs=[
            pl.BlockSpec((None, tl, OC_W), lambda b, r, i: (b, i, r * blocks_per_pos + q_blk)),
            pl.BlockSpec((None, L, OC_W), lambda b, r, i: (b, 0, r * blocks_per_pos + k_blk)),
            pl.BlockSpec((None, L, OC_W), lambda b, r, i: (b, 0, r * blocks_per_pos + v_blk)),
        ],
        out_specs=[
            pl.BlockSpec((None, tl, OC_W), lambda b, r, i: (b, i, r)),
            pl.BlockSpec((None, tl, OC_W), lambda b, r, i: (b, i, r)),
        ],
        out_shape=[jax.ShapeDtypeStruct((B, L, dil * OC_W), F32)] * 2,
        compiler_params=_params("parallel", "parallel", "arbitrary"),
        name=f"dilated_attention_g{group}",
    )(view, view, view)
    return o.reshape(B * S, OC_W), lse.reshape(B * S, OC_W)


def _mixture_kernel(o0, o1, o2, l0, l1, l2, out_ref):
    a0, a1, a2 = l0[...], l1[...], l2[...]
    m = jnp.maximum(jnp.maximum(a0, a1), a2)
    e0, e1, e2 = jnp.exp(a0 - m), jnp.exp(a1 - m), jnp.exp(a2 - m)
    den = e0 + e1 + e2
    out_ref[...] = ((e0 * o0[...] + e1 * o1[...] + e2 * o2[...]) / den).astype(out_ref.dtype)


def dilated_mixture(outs, lses):
    T = outs[0].shape[0]
    tm = 1024
    spec = pl.BlockSpec((tm, OC_W), lambda i: (i, 0))
    return pl.pallas_call(
        _mixture_kernel,
        grid=(T // tm,),
        in_specs=[spec] * 6,
        out_specs=spec,
        out_shape=jax.ShapeDtypeStruct((T, OC_W), BF16),
        compiler_params=_params("parallel"),
        name="dilated_mixture",
    )(*outs, *lses)


def _gqa_kernel(q_ref, k_ref, v_ref, o_ref):
    k = k_ref[...]
    v = v_ref[...]
    for g in range(D_GROUP):
        cols = slice(g * HEAD_DIM, (g + 1) * HEAD_DIM)
        s = _nt_dot(q_ref[:, cols], k) * SCALE
        m = jnp.max(s, axis=-1, keepdims=True)
        p = jnp.exp(s - m)
        l = jnp.sum(p, axis=-1, keepdims=True)
        o = jnp.dot(p.astype(BF16), v, preferred_element_type=F32) / l
        o_ref[:, cols] = o.astype(o_ref.dtype)


def gqa_attention(proj, B, S):
    tq = ATTN_TQ
    qw = D_GROUP * HEAD_DIM
    q0 = 3 * C_W // qw
    k0 = (3 * C_W + DQ_W) // HEAD_DIM
    v0 = k0 + D_KV_HEADS
    return pl.pallas_call(
        _gqa_kernel,
        grid=(D_KV_HEADS, B, S // tq),
        in_specs=[
            pl.BlockSpec((None, tq, qw), lambda h, b, i: (b, i, q0 + h)),
            pl.BlockSpec((None, S, HEAD_DIM), lambda h, b, i: (b, 0, k0 + h)),
            pl.BlockSpec((None, S, HEAD_DIM), lambda h, b, i: (b, 0, v0 + h)),
        ],
        out_specs=pl.BlockSpec((None, tq, qw), lambda h, b, i: (b, i, h)),
        out_shape=jax.ShapeDtypeStruct((B, S, DQ_W), BF16),
        compiler_params=_params("parallel", "parallel", "arbitrary"),
        name="gqa_attention",
    )(proj, proj, proj)


def _out_proj_kernel(a1_ref, a2_ref, w1_ref, w2_ref, h_ref, o_ref):
    acc = jnp.dot(a1_ref[...], w1_ref[...], preferred_element_type=F32)
    acc = acc + jnp.dot(a2_ref[...], w2_ref[...], preferred_element_type=F32)
    o_ref[...] = h_ref[...] + acc


def out_proj_residual(a1, a2, w1, w2, h):
    T, D = h.shape
    k1, k2 = a1.shape[1], a2.shape[1]
    tm, tn = PROJ_TM, PROJ_TN
    return pl.pallas_call(
        _out_proj_kernel,
        grid=(T // tm, D // tn),
        in_specs=[
            pl.BlockSpec((tm, k1), lambda i, j: (i, 0)),
            pl.BlockSpec((tm, k2), lambda i, j: (i, 0)),
            pl.BlockSpec((k1, tn), lambda i, j: (0, j)),
            pl.BlockSpec((k2, tn), lambda i, j: (0, j)),
            pl.BlockSpec((tm, tn), lambda i, j: (i, j)),
        ],
        out_specs=pl.BlockSpec((tm, tn), lambda i, j: (i, j)),
        out_shape=jax.ShapeDtypeStruct((T, D), F32),
        compiler_params=_params("parallel", "parallel"),
        name="out_proj_residual",
    )(a1, a2, w1, w2, h)


def _ffn_kernel(x_ref, g_ref, wg_ref, wu_ref, wd_ref, fg_ref, o_ref, xn_ref, *, final_norm):
    k = pl.program_id(1)

    @pl.when(k == 0)
    def _():
        _norm_rows_to(x_ref, g_ref, xn_ref, x_ref.shape[0])
        o_ref[...] = x_ref[...]

    xn = xn_ref[...]
    gate = jnp.dot(xn, wg_ref[...], preferred_element_type=F32)
    up = jnp.dot(xn, wu_ref[...], preferred_element_type=F32)
    act = (gate * jax.nn.sigmoid(gate)) * up
    o_ref[...] += jnp.dot(act.astype(BF16), wd_ref[...], preferred_element_type=F32)

    if final_norm:
        @pl.when(k == pl.num_programs(1) - 1)
        def _():
            fg = fg_ref[...]

            def body(c, carry):
                r = pl.multiple_of(c * NORM_ROW_CHUNK, NORM_ROW_CHUNK)
                o_ref[pl.ds(r, NORM_ROW_CHUNK), :] = _rms(o_ref[pl.ds(r, NORM_ROW_CHUNK), :], fg)
                return carry

            lax.fori_loop(0, o_ref.shape[0] // NORM_ROW_CHUNK, body, 0)


def ffn_residual(x, g, wg, wu, wd, final_g, final_norm):
    T, D = x.shape
    H = wg.shape[1]
    tm, th = FFN_TM, FFN_TH
    kernel = functools.partial(_ffn_kernel, final_norm=final_norm)
    return pl.pallas_call(
        kernel,
        grid=(T // tm, H // th),
        in_specs=[
            pl.BlockSpec((tm, D), lambda i, k: (i, 0)),
            pl.BlockSpec((1, D), lambda i, k: (0, 0)),
            pl.BlockSpec((D, th), lambda i, k: (0, k)),
            pl.BlockSpec((D, th), lambda i, k: (0, k)),
            pl.BlockSpec((th, D), lambda i, k: (k, 0)),
            pl.BlockSpec((1, D), lambda i, k: (0, 0)),
        ],
        out_specs=pl.BlockSpec((tm, D), lambda i, k: (i, 0)),
        out_shape=jax.ShapeDtypeStruct((T, D), F32),
        scratch_shapes=[pltpu.VMEM((tm, D), BF16)],
        compiler_params=_params("parallel", "arbitrary"),
        name="ffn_residual",
    )(x, g.reshape(1, D), wg, wu, wd, final_g.reshape(1, D))


def _rope_tables(S):
    t = jnp.arange(S)
    row = (t // GRID_W).astype(F32)
    col = (t % GRID_W).astype(F32)
    half = HEAD_DIM // 2
    f_row = ROPE_THETA ** (-jnp.arange(0, half, 2, dtype=F32) / half)
    f_col = ROPE_THETA ** (-jnp.arange(0, HEAD_DIM - half, 2, dtype=F32) / (HEAD_DIM - half))
    ang = jnp.concatenate([row[:, None] * f_row[None, :], col[:, None] * f_col[None, :]], axis=-1)
    cos = jnp.repeat(jnp.cos(ang), 2, axis=-1)
    sin = jnp.repeat(jnp.sin(ang), 2, axis=-1)
    sign = jnp.where(jnp.arange(HEAD_DIM) % 2 == 0, -1.0, 1.0).astype(F32)
    return cos, sin * sign[None, :]


def _lambda_init(layer_idx):
    return 0.8 - 0.6 * math.exp(-0.3 * layer_idx)


def _trunk(x, w):
    B, S, D = x.shape
    T = B * S
    h = x.reshape(T, D)
    cos, sin_signed = _rope_tables(S)
    for i in range(DEPTH):
        j = i // 2
        if i % 2 == 0:
            proj = norm_proj(h, w["attn_norm_g"][i], w["ev_w_in"][j])
            proj3 = proj.reshape(B, S, EV_IN)
            oa = diff_attention(proj3, w["ev_lambda"][j], w["ev_subln_g"][j], B, S, _lambda_init(i))
            ob = neighbourhood_attention(proj3, w["ev_rpb"][j], B, S)
            h = out_proj_residual(oa.reshape(T, A_W), ob.reshape(T, B_W),
                                  w["ev_w_out"][j][0], w["ev_w_out"][j][1], h)
        else:
            proj = od_norm_proj(h, w["attn_norm_g"][i], w["od_w_in"][j], cos, sin_signed,
                                w["od_qk_norm_g"][j].astype(F32), S)
            groups = [dilated_group(proj, B, S, g) for g in range(len(C_PATTERNS))]
            oc = dilated_mixture([o for o, _ in groups], [l for _, l in groups])
            od = gqa_attention(proj.reshape(B, S, OD_IN), B, S)
            h = out_proj_residual(oc, od.reshape(T, DQ_W), w["od_w_out"][j][0], w["od_w_out"][j][1], h)
        h = ffn_residual(h, w["ffn_norm_g"][i], w["ffn_w_gate"][i], w["ffn_w_up"][i], w["ffn_w_down"][i],
                         w["final_norm_g"], final_norm=(i == DEPTH - 1))
    return h.reshape(B, S, D)


def kernel(x_prompt, x_sample, attn_norm_g, ev_w_in, ev_lambda, ev_subln_g, ev_rpb, ev_w_out, od_w_in,
           od_qk_norm_g, od_w_out, ffn_norm_g, ffn_w_gate, ffn_w_up, ffn_w_down, final_norm_g):
    w = {
        "attn_norm_g": attn_norm_g, "ffn_norm_g": ffn_norm_g, "final_norm_g": final_norm_g,
        "ev_lambda": ev_lambda, "ev_subln_g": ev_subln_g, "ev_rpb": ev_rpb, "od_qk_norm_g": od_qk_norm_g,
        "ev_w_in": [ev_w_in[j].astype(BF16) for j in range(ev_w_in.shape[0])],
        "ev_w_out": [(ev_w_out[j, :A_W].astype(BF16), ev_w_out[j, A_W:].astype(BF16))
                     for j in range(ev_w_out.shape[0])],
        "od_w_in": [od_w_in[j].astype(BF16) for j in range(od_w_in.shape[0])],
        "od_w_out": [(od_w_out[j, :OC_W].astype(BF16), od_w_out[j, OC_W:].astype(BF16))
                     for j in range(od_w_out.shape[0])],
        "ffn_w_gate": [ffn_w_gate[i].astype(BF16) for i in range(DEPTH)],
        "ffn_w_up": [ffn_w_up[i].astype(BF16) for i in range(DEPTH)],
        "ffn_w_down": [ffn_w_down[i].astype(BF16) for i in range(DEPTH)],
    }
    return _trunk(x_prompt, w), _trunk(x_sample, w)
```

```python
import functools
import math

import jax
import jax.numpy as jnp
from jax import lax
from jax.experimental import pallas as pl
from jax.experimental.pallas import tpu as pltpu

F32 = jnp.float32
BF16 = jnp.bfloat16

D_MODEL = 2048
DEPTH = 2
HEAD_DIM = 128
GRID_W = 64
EPS = 1e-6
NEG_INF = -1e30
SCALE = HEAD_DIM ** -0.5

A_HEADS = 4
B_HEADS = 8
NA_KH = 8
NA_KW = 16
C_PATTERNS = ((128, 1), (512, 4), (2048, 16))
C_GROUP_HEADS = 4
C_HEADS = 12
D_HEADS = 12
D_KV_HEADS = 4
D_GROUP = D_HEADS // D_KV_HEADS
ROPE_THETA = 10000.0
FFN_HIDDEN = 5632

A_W = A_HEADS * 2 * HEAD_DIM
B_W = B_HEADS * HEAD_DIM
EV_IN = 3 * A_W + 3 * B_W
C_W = C_HEADS * HEAD_DIM
DQ_W = D_HEADS * HEAD_DIM
DKV_W = D_KV_HEADS * HEAD_DIM
OD_IN = 3 * C_W + DQ_W + 2 * DKV_W
OC_W = C_GROUP_HEADS * HEAD_DIM

VMEM_LIMIT_BYTES = 56 * 1024 * 1024
LANES = 128

PROJ_TM = 1024
PROJ_TN = 1024
FFN_TM = 1024
FFN_TH = 512
NORM_ROW_CHUNK = 256
ATTN_TQ = 256
DIL_RADIUS = 64
OUT_PROJ_TM = 512


def _params(*sem):
    return pltpu.CompilerParams(dimension_semantics=sem, vmem_limit_bytes=VMEM_LIMIT_BYTES)


def _nt_dot(a, b):
    return lax.dot_general(a, b, (((1,), (1,)), ((), ())), preferred_element_type=F32)


def _rms(x, g):
    ms = jnp.mean(x * x, axis=-1, keepdims=True)
    return (x * lax.rsqrt(ms + EPS)) * g


def _norm_rows_to(x_ref, g_ref, xn_ref, rows):
    g = g_ref[...]

    def body(c, carry):
        r = pl.multiple_of(c * NORM_ROW_CHUNK, NORM_ROW_CHUNK)
        xn_ref[pl.ds(r, NORM_ROW_CHUNK), :] = _rms(x_ref[pl.ds(r, NORM_ROW_CHUNK), :], g).astype(BF16)
        return carry

    lax.fori_loop(0, rows // NORM_ROW_CHUNK, body, 0)


def _norm_proj_kernel(x_ref, g_ref, w_ref, o_ref, xn_ref):
    @pl.when(pl.program_id(1) == 0)
    def _():
        _norm_rows_to(x_ref, g_ref, xn_ref, x_ref.shape[0])

    o_ref[...] = jnp.dot(xn_ref[...], w_ref[...], preferred_element_type=F32).astype(o_ref.dtype)


def norm_proj(x, g, w):
    T, D = x.shape
    N = w.shape[1]
    tm, tn = PROJ_TM, PROJ_TN
    return pl.pallas_call(
        _norm_proj_kernel,
        grid=(T // tm, N // tn),
        in_specs=[
            pl.BlockSpec((tm, D), lambda i, j: (i, 0)),
            pl.BlockSpec((1, D), lambda i, j: (0, 0)),
            pl.BlockSpec((D, tn), lambda i, j: (0, j)),
        ],
        out_specs=pl.BlockSpec((tm, tn), lambda i, j: (i, j)),
        out_shape=jax.ShapeDtypeStruct((T, N), BF16),
        scratch_shapes=[pltpu.VMEM((tm, D), BF16)],
        compiler_params=_params("parallel", "arbitrary"),
        name="norm_proj",
    )(x, g.reshape(1, D), w)


def _qk_norm_rope(x, g, cos, sin_signed):
    y = _rms(x, g)
    lane = lax.broadcasted_iota(jnp.int32, y.shape, 1)
    nxt = pltpu.roll(y, LANES - 1, 1)
    prv = pltpu.roll(y, 1, 1)
    partner = jnp.where((lane & 1) == 0, nxt, prv)
    return y * cos + partner * sin_signed


def _od_head_kinds(tn):
    slots = tn // LANES
    q_lo, q_hi = 3 * C_W // LANES, (3 * C_W + DQ_W) // LANES
    k_hi = q_hi + DKV_W // LANES
    table = {}
    for j in range(OD_IN // tn):
        kinds = []
        for c in range(slots):
            s = j * slots + c
            kinds.append(0 if q_lo <= s < q_hi else (1 if q_hi <= s < k_hi else None))
        if any(k is not None for k in kinds):
            table[j] = kinds
    return table


def _od_proj_kernel(x_ref, g_ref, w_ref, cos_ref, sin_ref, qkg_ref, o_ref, xn_ref, *, kinds_by_tile):
    j = pl.program_id(1)

    @pl.when(j == 0)
    def _():
        _norm_rows_to(x_ref, g_ref, xn_ref, x_ref.shape[0])

    acc = jnp.dot(xn_ref[...], w_ref[...], preferred_element_type=F32)

    plain = None
    for jj, kinds in kinds_by_tile.items():
        hit = j == jj
        plain = jnp.logical_not(hit) if plain is None else jnp.logical_and(plain, jnp.logical_not(hit))

        @pl.when(hit)
        def _(kinds=kinds):
            cos = cos_ref[...]
            sin = sin_ref[...]
            for c, kind in enumerate(kinds):
                blk = acc[:, c * LANES:(c + 1) * LANES]
                if kind is not None:
                    blk = _qk_norm_rope(blk, qkg_ref[kind:kind + 1, :], cos, sin)
                o_ref[:, c * LANES:(c + 1) * LANES] = blk.astype(o_ref.dtype)

    @pl.when(plain)
    def _():
        o_ref[...] = acc.astype(o_ref.dtype)


def od_norm_proj(x, g, w, cos, sin_signed, qk_g, S):
    T, D = x.shape
    N = w.shape[1]
    tm, tn = PROJ_TM, PROJ_TN
    s_tiles = S // tm
    kernel = functools.partial(_od_proj_kernel, kinds_by_tile=_od_head_kinds(tn))
    return pl.pallas_call(
        kernel,
        grid=(T // tm, N // tn),
        in_specs=[
            pl.BlockSpec((tm, D), lambda i, j: (i, 0)),
            pl.BlockSpec((1, D), lambda i, j: (0, 0)),
            pl.BlockSpec((D, tn), lambda i, j: (0, j)),
            pl.BlockSpec((tm, HEAD_DIM), lambda i, j: (i % s_tiles, 0)),
            pl.BlockSpec((tm, HEAD_DIM), lambda i, j: (i % s_tiles, 0)),
            pl.BlockSpec((2, HEAD_DIM), lambda i, j: (0, 0)),
        ],
        out_specs=pl.BlockSpec((tm, tn), lambda i, j: (i, j)),
        out_shape=jax.ShapeDtypeStruct((T, N), BF16),
        scratch_shapes=[pltpu.VMEM((tm, D), BF16)],
        compiler_params=_params("parallel", "arbitrary"),
        name="od_norm_proj",
    )(x, g.reshape(1, D), w, cos, sin_signed, qk_g)


def _diff_attn_kernel(slopes_ref, lam_ref, g_ref, q_ref, k_ref, v_ref, o_ref, bias_ref, *, S, tq, lambda_init):
    h = pl.program_id(0)
    b = pl.program_id(1)
    i = pl.program_id(2)
    width = 2 * S - tq
    chunk = tq

    @pl.when(jnp.logical_and(b == 0, i == 0))
    def _():
        slope = slopes_ref[h]
        row = lax.broadcasted_iota(jnp.int32, (tq, chunk), 0)
        col = lax.broadcasted_iota(jnp.int32, (tq, chunk), 1)
        base = row + (S - tq) - col

        def body(c, carry):
            off = pl.multiple_of(c * chunk, chunk)
            dist = jnp.abs(base - off).astype(F32)
            bias_ref[:, pl.ds(off, chunk)] = -(slope * dist)
            return carry

        lax.fori_loop(0, width // chunk, body, 0)

    start = pl.multiple_of(S - tq - i * tq, LANES)
    lv = lam_ref[...]
    lam = (jnp.exp(jnp.sum(lv[0:1] * lv[1:2], axis=-1, keepdims=True))
           - jnp.exp(jnp.sum(lv[2:3] * lv[3:4], axis=-1, keepdims=True)) + lambda_init)

    def component(c):
        q = q_ref[:, c * HEAD_DIM:(c + 1) * HEAD_DIM]
        k = k_ref[:, c * HEAD_DIM:(c + 1) * HEAD_DIM]
        s = _nt_dot(q, k) * SCALE + bias_ref[:, pl.ds(start, S)]
        m = jnp.max(s, axis=-1, keepdims=True)
        p = jnp.exp(s - m)
        l = jnp.sum(p, axis=-1, keepdims=True)
        o = jnp.dot(p.astype(BF16), v_ref[...], preferred_element_type=F32)
        return o / l

    o = component(0) - lam * component(1)
    o = _rms(o, g_ref[...]) * (1.0 - lambda_init)
    o_ref[...] = o.astype(o_ref.dtype)


def diff_attention(proj, lam_vec, subln_g, B, S, lambda_init):
    tq = ATTN_TQ
    dv = 2 * HEAD_DIM
    slopes = jnp.asarray([2.0 ** (-8.0 * (i + 1) / A_HEADS) for i in range(A_HEADS)], F32)
    kernel = functools.partial(_diff_attn_kernel, S=S, tq=tq, lambda_init=lambda_init)
    return pl.pallas_call(
        kernel,
        grid=(A_HEADS, B, S // tq),
        in_specs=[
            pl.BlockSpec(memory_space=pltpu.SMEM),
            pl.BlockSpec((4, HEAD_DIM), lambda h, b, i: (0, 0)),
            pl.BlockSpec((1, dv), lambda h, b, i: (0, 0)),
            pl.BlockSpec((None, tq, dv), lambda h, b, i: (b, i, h)),
            pl.BlockSpec((None, S, dv), lambda h, b, i: (b, 0, A_HEADS + h)),
            pl.BlockSpec((None, S, dv), lambda h, b, i: (b, 0, 2 * A_HEADS + h)),
        ],
        out_specs=pl.BlockSpec((None, tq, dv), lambda h, b, i: (b, i, h)),
        out_shape=jax.ShapeDtypeStruct((B, S, A_W), BF16),
        scratch_shapes=[pltpu.VMEM((tq, 2 * S - tq), F32)],
        compiler_params=_params("arbitrary", "arbitrary", "arbitrary"),
        name="diff_attention",
    )(slopes, lam_vec.astype(F32), subln_g.reshape(1, dv), proj, proj, proj)


RPB_ROWS = 2 * NA_KH - 1
RPB_COLS = 2 * NA_KW - 1
NA_Q_ROWS = 4
NA_K_ROWS = 12
NA_Q = NA_Q_ROWS * GRID_W
NA_K = NA_K_ROWS * GRID_W


def _na_case_rule(case, a):
    if case == 0:
        return 0, NA_KH, NA_KH - 1 - a
    if case == 1:
        return a, a + NA_KH, NA_KH // 2 - 1 - a
    return NA_K_ROWS - NA_KH, NA_K_ROWS, -1 - a


def _nbr_attn_kernel(rpb_ref, q_ref, k_ref, v_ref, o_ref, pair_ref, tab_ref, *, S):
    h = pl.program_id(0)
    b = pl.program_id(1)
    rows = S // GRID_W
    n_groups = rows // NA_Q_ROWS

    @pl.when(b == 0)
    def _():
        c = lax.broadcasted_iota(jnp.int32, (GRID_W, LANES), 0)
        lane = lax.broadcasted_iota(jnp.int32, (GRID_W, LANES), 1)
        w = lane & (GRID_W - 1)
        upper = lane >= GRID_W
        c0 = jnp.clip(c - NA_KW // 2, 0, GRID_W - NA_KW)
        valid = jnp.logical_and(w >= c0, w < c0 + NA_KW)
        dc = jnp.clip(w - c, 1 - NA_KW, NA_KW - 1) + NA_KW - 1
        base = h * (RPB_ROWS * RPB_COLS)
        for e in range(RPB_ROWS + 1):
            def body(d, acc, e=e):
                lo = rpb_ref[base + (e - 1) * RPB_COLS + d] if e >= 1 else 0.0
                hi = rpb_ref[base + e * RPB_COLS + d] if e < RPB_ROWS else 0.0
                return jnp.where(dc == d, jnp.where(upper, hi, lo), acc)

            acc = lax.fori_loop(0, RPB_COLS, body, jnp.zeros((GRID_W, LANES), F32))
            pair_ref[e] = jnp.where(valid, acc, NEG_INF)

        for case in range(3):
            for a in range(NA_Q_ROWS):
                lo_slot, hi_slot, off = _na_case_rule(case, a)
                for t in range(NA_K_ROWS // 2):
                    see0 = lo_slot <= 2 * t < hi_slot
                    see1 = lo_slot <= 2 * t + 1 < hi_slot
                    if see0 or see1:
                        tile = pair_ref[2 * t + off + 1]
                        if not see0:
                            tile = jnp.where(upper, tile, NEG_INF)
                        if not see1:
                            tile = jnp.where(upper, NEG_INF, tile)
                    else:
                        tile = jnp.full((GRID_W, LANES), NEG_INF, F32)
                    tab_ref[case, a * GRID_W:(a + 1) * GRID_W, t * LANES:(t + 1) * LANES] = tile

    def group(gi):
        case = jnp.where(gi == 0, 0, jnp.where(gi == n_groups - 1, 2, 1))
        u0 = jnp.clip(gi * NA_Q_ROWS - NA_KH // 2, 0, rows - NA_K_ROWS)
        qs = pl.multiple_of(gi * NA_Q, NA_Q)
        ks = pl.multiple_of(u0 * GRID_W, GRID_W)
        q = q_ref[pl.ds(qs, NA_Q), :]
        k = k_ref[pl.ds(ks, NA_K), :]
        v = v_ref[pl.ds(ks, NA_K), :]
        s = _nt_dot(q, k) * SCALE + tab_ref[case]
        m = jnp.max(s, axis=-1, keepdims=True)
        p = jnp.exp(s - m)
        l = jnp.sum(p, axis=-1, keepdims=True)
        o = jnp.dot(p.astype(BF16), v, preferred_element_type=F32) / l
        o_ref[pl.ds(qs, NA_Q), :] = o.astype(o_ref.dtype)

    def step(gp, carry):
        group(2 * gp)
        group(2 * gp + 1)
        return carry

    lax.fori_loop(0, n_groups // 2, step, 0)


def neighbourhood_attention(proj, rpb, B, S):
    q0 = 3 * A_W // HEAD_DIM
    kernel = functools.partial(_nbr_attn_kernel, S=S)
    return pl.pallas_call(
        kernel,
        grid=(B_HEADS, B),
        in_specs=[
            pl.BlockSpec(memory_space=pltpu.SMEM),
            pl.BlockSpec((None, S, HEAD_DIM), lambda h, b: (b, 0, q0 + h)),
            pl.BlockSpec((None, S, HEAD_DIM), lambda h, b: (b, 0, q0 + B_HEADS + h)),
            pl.BlockSpec((None, S, HEAD_DIM), lambda h, b: (b, 0, q0 + 2 * B_HEADS + h)),
        ],
        out_specs=pl.BlockSpec((None, S, HEAD_DIM), lambda h, b: (b, 0, h)),
        out_shape=jax.ShapeDtypeStruct((B, S, B_W), BF16),
        scratch_shapes=[pltpu.VMEM((RPB_ROWS + 1, GRID_W, LANES), F32),
                        pltpu.VMEM((3, NA_Q, NA_K), F32)],
        compiler_params=_params("arbitrary", "arbitrary"),
        name="neighbourhood_attention",
    )(rpb.astype(F32).reshape(-1), proj, proj, proj)


def _dil_geometry(S, tl, dil):
    half = DIL_RADIUS * dil
    hp = -(-half // LANES) * LANES
    kw = min(S, tl + 2 * hp)
    deltas = [min(max(l0 - hp, 0), S - kw) - l0 for l0 in range(0, S, tl)]
    u0 = -min(deltas)
    return half, hp, kw, u0, kw + max(deltas) + u0


def _dilated_kernel(slopes_ref, q0_ref, q1_ref, q2_ref, k0_ref, k1_ref, k2_ref, v0_ref, v1_ref, v2_ref,
                    o_ref, b0_ref, b1_ref, b2_ref, *, S, tl, geoms):
    hh = pl.program_id(0)
    b = pl.program_id(1)
    i = pl.program_id(2)
    q_refs, k_refs, v_refs = (q0_ref, q1_ref, q2_ref), (k0_ref, k1_ref, k2_ref), (v0_ref, v1_ref, v2_ref)
    bias_refs = (b0_ref, b1_ref, b2_ref)

    @pl.when(jnp.logical_and(b == 0, i == 0))
    def _():
        for g, (_, dil) in enumerate(C_PATTERNS):
            half, _, _, u0, width = geoms[g]
            slope = slopes_ref[g * C_GROUP_HEADS + hh]
            row = lax.broadcasted_iota(jnp.int32, (tl, LANES), 0)
            col = lax.broadcasted_iota(jnp.int32, (tl, LANES), 1)
            base = col - row - u0

            def body(c, carry, g=g, dil=dil, half=half, slope=slope, base=base):
                off = pl.multiple_of(c * LANES, LANES)
                rel = base + off
                dist = jnp.abs(rel)
                ok = jnp.logical_and(dist <= half, (rel & (dil - 1)) == 0)
                bias_refs[g][:, pl.ds(off, LANES)] = jnp.where(ok, -(slope * dist.astype(F32)), NEG_INF)
                return carry

            lax.fori_loop(0, width // LANES, body, 0)

    l0 = i * tl
    accs, ms, ls = [], [], []
    for g in range(len(C_PATTERNS)):
        _, hp, kw, u0, _ = geoms[g]
        ws = pl.multiple_of(jnp.clip(l0 - hp, 0, S - kw), LANES)
        start = pl.multiple_of(ws - l0 + u0, LANES)
        k = k_refs[g][pl.ds(ws, kw), :]
        v = v_refs[g][pl.ds(ws, kw), :]
        s = _nt_dot(q_refs[g][...], k) * SCALE + bias_refs[g][:, pl.ds(start, kw)]
        m = jnp.max(s, axis=-1, keepdims=True)
        p = jnp.exp(s - m)
        ls.append(jnp.sum(p, axis=-1, keepdims=True))
        ms.append(m)
        accs.append(jnp.dot(p.astype(BF16), v, preferred_element_type=F32))

    m_all = jnp.maximum(jnp.maximum(ms[0], ms[1]), ms[2])
    ws_ = [jnp.exp(m - m_all) for m in ms]
    num = ws_[0] * accs[0] + ws_[1] * accs[1] + ws_[2] * accs[2]
    den = ws_[0] * ls[0] + ws_[1] * ls[1] + ws_[2] * ls[2]
    o_ref[...] = (num / den).astype(o_ref.dtype)


def dilated_mixture_attention(proj, B, S):
    tl = ATTN_TQ
    ng = len(C_PATTERNS)
    for win, dil in C_PATTERNS:
        assert win // (2 * dil) == DIL_RADIUS and dil & (dil - 1) == 0
    geoms = tuple(_dil_geometry(S, tl, dil) for _, dil in C_PATTERNS)
    slopes = jnp.asarray([2.0 ** (-8.0 * (i + 1) / C_HEADS) for i in range(C_HEADS)], F32)
    k0, v0 = C_W // HEAD_DIM, 2 * C_W // HEAD_DIM

    def q_spec(g):
        return pl.BlockSpec((None, tl, HEAD_DIM), lambda hh, b, i: (b, i, g * C_GROUP_HEADS + hh))

    def kv_spec(first, g):
        return pl.BlockSpec((None, S, HEAD_DIM), lambda hh, b, i: (b, 0, first + g * C_GROUP_HEADS + hh))

    kernel = functools.partial(_dilated_kernel, S=S, tl=tl, geoms=geoms)
    return pl.pallas_call(
        kernel,
        grid=(C_GROUP_HEADS, B, S // tl),
        in_specs=([pl.BlockSpec(memory_space=pltpu.SMEM)]
                  + [q_spec(g) for g in range(ng)]
                  + [kv_spec(k0, g) for g in range(ng)]
                  + [kv_spec(v0, g) for g in range(ng)]),
        out_specs=pl.BlockSpec((None, tl, HEAD_DIM), lambda hh, b, i: (b, i, hh)),
        out_shape=jax.ShapeDtypeStruct((B, S, OC_W), BF16),
        scratch_shapes=[pltpu.VMEM((tl, geoms[g][4]), F32) for g in range(ng)],
        compiler_params=_params("arbitrary", "arbitrary", "arbitrary"),
        name="dilated_mixture_attention",
    )(slopes, *([proj] * (3 * ng)))


def _gqa_kernel(q_ref, k_ref, v_ref, o_ref):
    k = k_ref[...]
    v = v_ref[...]
    for g in range(D_GROUP):
        cols = slice(g * HEAD_DIM, (g + 1) * HEAD_DIM)
        s = _nt_dot(q_ref[:, cols], k) * SCALE
        m = jnp.max(s, axis=-1, keepdims=True)
        p = jnp.exp(s - m)
        l = jnp.sum(p, axis=-1, keepdims=True)
        o = jnp.dot(p.astype(BF16), v, preferred_element_type=F32) / l
        o_ref[:, cols] = o.astype(o_ref.dtype)


def gqa_attention(proj, B, S):
    tq = ATTN_TQ
    qw = D_GROUP * HEAD_DIM
    q0 = 3 * C_W // qw
    k0 = (3 * C_W + DQ_W) // HEAD_DIM
    v0 = k0 + D_KV_HEADS
    return pl.pallas_call(
        _gqa_kernel,
        grid=(D_KV_HEADS, B, S // tq),
        in_specs=[
            pl.BlockSpec((None, tq, qw), lambda h, b, i: (b, i, q0 + h)),
            pl.BlockSpec((None, S, HEAD_DIM), lambda h, b, i: (b, 0, k0 + h)),
            pl.BlockSpec((None, S, HEAD_DIM), lambda h, b, i: (b, 0, v0 + h)),
        ],
        out_specs=pl.BlockSpec((None, tq, qw), lambda h, b, i: (b, i, h)),
        out_shape=jax.ShapeDtypeStruct((B, S, DQ_W), BF16),
        compiler_params=_params("parallel", "parallel", "arbitrary"),
        name="gqa_attention",
    )(proj, proj, proj)


def _out_proj_kernel(a1_ref, a2_ref, w1_ref, w2_ref, h_ref, o_ref):
    acc = jnp.dot(a1_ref[...], w1_ref[...], preferred_element_type=F32)
    acc = acc + jnp.dot(a2_ref[...], w2_ref[...], preferred_element_type=F32)
    o_ref[...] = h_ref[...] + acc


def out_proj_residual(a1, a2, w1, w2, h):
    T, D = h.shape
    k1, k2 = a1.shape[1], a2.shape[1]
    tm = OUT_PROJ_TM
    resident = pl.Buffered(1)
    return pl.pallas_call(
        _out_proj_kernel,
        grid=(T // tm,),
        in_specs=[
            pl.BlockSpec((tm, k1), lambda i: (i, 0)),
            pl.BlockSpec((tm, k2), lambda i: (i, 0)),
            pl.BlockSpec((k1, D), lambda i: (0, 0), pipeline_mode=resident),
            pl.BlockSpec((k2, D), lambda i: (0, 0), pipeline_mode=resident),
            pl.BlockSpec((tm, D), lambda i: (i, 0)),
        ],
        out_specs=pl.BlockSpec((tm, D), lambda i: (i, 0)),
        out_shape=jax.ShapeDtypeStruct((T, D), F32),
        compiler_params=_params("parallel"),
        name="out_proj_residual",
    )(a1, a2, w1, w2, h)


def _ffn_kernel(x_ref, g_ref, wg_ref, wu_ref, wd_ref, fg_ref, o_ref, xn_ref, *, final_norm):
    k = pl.program_id(1)

    @pl.when(k == 0)
    def _():
        _norm_rows_to(x_ref, g_ref, xn_ref, x_ref.shape[0])
        o_ref[...] = x_ref[...]

    xn = xn_ref[...]
    gate = jnp.dot(xn, wg_ref[...], preferred_element_type=F32)
    up = jnp.dot(xn, wu_ref[...], preferred_element_type=F32)
    act = (gate * jax.nn.sigmoid(gate)) * up
    o_ref[...] += jnp.dot(act.astype(BF16), wd_ref[...], preferred_element_type=F32)

    if final_norm:
        @pl.when(k == pl.num_programs(1) - 1)
        def _():
            fg = fg_ref[...]

            def body(c, carry):
                r = pl.multiple_of(c * NORM_ROW_CHUNK, NORM_ROW_CHUNK)
                o_ref[pl.ds(r, NORM_ROW_CHUNK), :] = _rms(o_ref[pl.ds(r, NORM_ROW_CHUNK), :], fg)
                return carry

            lax.fori_loop(0, o_ref.shape[0] // NORM_ROW_CHUNK, body, 0)


def ffn_residual(x, g, wg, wu, wd, final_g, final_norm):
    T, D = x.shape
    H = wg.shape[1]
    tm, th = FFN_TM, FFN_TH
    kernel = functools.partial(_ffn_kernel, final_norm=final_norm)
    return pl.pallas_call(
        kernel,
        grid=(T // tm, H // th),
        in_specs=[
            pl.BlockSpec((tm, D), lambda i, k: (i, 0)),
            pl.BlockSpec((1, D), lambda i, k: (0, 0)),
            pl.BlockSpec((D, th), lambda i, k: (0, k)),
            pl.BlockSpec((D, th), lambda i, k: (0, k)),
            pl.BlockSpec((th, D), lambda i, k: (k, 0)),
            pl.BlockSpec((1, D), lambda i, k: (0, 0)),
        ],
        out_specs=pl.BlockSpec((tm, D), lambda i, k: (i, 0)),
        out_shape=jax.ShapeDtypeStruct((T, D), F32),
        scratch_shapes=[pltpu.VMEM((tm, D), BF16)],
        compiler_params=_params("parallel", "arbitrary"),
        name="ffn_residual",
    )(x, g.reshape(1, D), wg, wu, wd, final_g.reshape(1, D))


def _rope_tables(S):
    t = jnp.arange(S)
    row = (t // GRID_W).astype(F32)
    col = (t % GRID_W).astype(F32)
    half = HEAD_DIM // 2
    f_row = ROPE_THETA ** (-jnp.arange(0, half, 2, dtype=F32) / half)
    f_col = ROPE_THETA ** (-jnp.arange(0, HEAD_DIM - half, 2, dtype=F32) / (HEAD_DIM - half))
    ang = jnp.concatenate([row[:, None] * f_row[None, :], col[:, None] * f_col[None, :]], axis=-1)
    cos = jnp.repeat(jnp.cos(ang), 2, axis=-1)
    sin = jnp.repeat(jnp.sin(ang), 2, axis=-1)
    sign = jnp.where(jnp.arange(HEAD_DIM) % 2 == 0, -1.0, 1.0).astype(F32)
    return cos, sin * sign[None, :]


def _lambda_init(layer_idx):
    return 0.8 - 0.6 * math.exp(-0.3 * layer_idx)


def _trunk(x, w):
    B, S, D = x.shape
    T = B * S
    h = x.reshape(T, D)
    cos, sin_signed = _rope_tables(S)
    for i in range(DEPTH):
        j = i // 2
        if i % 2 == 0:
            proj = norm_proj(h, w["attn_norm_g"][i], w["ev_w_in"][j])
            proj3 = proj.reshape(B, S, EV_IN)
            oa = diff_attention(proj3, w["ev_lambda"][j], w["ev_subln_g"][j], B, S, _lambda_init(i))
            ob = neighbourhood_attention(proj3, w["ev_rpb"][j], B, S)
            h = out_proj_residual(oa.reshape(T, A_W), ob.reshape(T, B_W),
                                  w["ev_w_out"][j][0], w["ev_w_out"][j][1], h)
        else:
            proj = od_norm_proj(h, w["attn_norm_g"][i], w["od_w_in"][j], cos, sin_signed,
                                w["od_qk_norm_g"][j].astype(F32), S)
            proj3 = proj.reshape(B, S, OD_IN)
            oc = dilated_mixture_attention(proj3, B, S).reshape(T, OC_W)
            od = gqa_attention(proj3, B, S)
            h = out_proj_residual(oc, od.reshape(T, DQ_W), w["od_w_out"][j][0], w["od_w_out"][j][1], h)
        h = ffn_residual(h, w["ffn_norm_g"][i], w["ffn_w_gate"][i], w["ffn_w_up"][i], w["ffn_w_down"][i],
                         w["final_norm_g"], final_norm=(i == DEPTH - 1))
    return h.reshape(B, S, D)


def kernel(x_prompt, x_sample, attn_norm_g, ev_w_in, ev_lambda, ev_subln_g, ev_rpb, ev_w_out, od_w_in,
           od_qk_norm_g, od_w_out, ffn_norm_g, ffn_w_gate, ffn_w_up, ffn_w_down, final_norm_g):
    w = {
        "attn_norm_g": attn_norm_g, "ffn_norm_g": ffn_norm_g, "final_norm_g": final_norm_g,
        "ev_lambda": ev_lambda, "ev_subln_g": ev_subln_g, "ev_rpb": ev_rpb, "od_qk_norm_g": od_qk_norm_g,
        "ev_w_in": [ev_w_in[j].astype(BF16) for j in range(ev_w_in.shape[0])],
        "ev_w_out": [(ev_w_out[j, :A_W].astype(BF16), ev_w_out[j, A_W:].astype(BF16))
                     for j in range(ev_w_out.shape[0])],
        "od_w_in": [od_w_in[j].astype(BF16) for j in range(od_w_in.shape[0])],
        "od_w_out": [(od_w_out[j, :OC_W].astype(BF16), od_w_out[j, OC_W:].astype(BF16))
                     for j in range(od_w_out.shape[0])],
        "ffn_w_gate": [ffn_w_gate[i].astype(BF16) for i in range(DEPTH)],
        "ffn_w_up": [ffn_w_up[i].astype(BF16) for i in range(DEPTH)],
        "ffn_w_down": [ffn_w_down[i].astype(BF16) for i in range(DEPTH)],
    }
    return _trunk(x_prompt, w), _trunk(x_sample, w)
```

```python
import functools
import math

import jax
import jax.numpy as jnp
from jax import lax
from jax.experimental import pallas as pl
from jax.experimental.pallas import tpu as pltpu

F32 = jnp.float32
BF16 = jnp.bfloat16

D_MODEL = 2048
DEPTH = 2
HEAD_DIM = 128
GRID_W = 64
EPS = 1e-6
NEG_INF = -1e30
SCALE = HEAD_DIM ** -0.5
LOG2E = math.log2(math.e)
Q_PRESCALE = SCALE * LOG2E

A_HEADS = 4
B_HEADS = 8
NA_KH = 8
NA_KW = 16
C_PATTERNS = ((128, 1), (512, 4), (2048, 16))
C_GROUP_HEADS = 4
C_HEADS = 12
D_HEADS = 12
D_KV_HEADS = 4
D_GROUP = D_HEADS // D_KV_HEADS
ROPE_THETA = 10000.0
FFN_HIDDEN = 5632

A_W = A_HEADS * 2 * HEAD_DIM
B_W = B_HEADS * HEAD_DIM
EV_IN = 3 * A_W + 3 * B_W
C_W = C_HEADS * HEAD_DIM
DQ_W = D_HEADS * HEAD_DIM
DKV_W = D_KV_HEADS * HEAD_DIM
OD_IN = 3 * C_W + DQ_W + 2 * DKV_W
OC_W = C_GROUP_HEADS * HEAD_DIM

VMEM_LIMIT_BYTES = 56 * 1024 * 1024
LANES = 128

PROJ_TM = 1024
PROJ_TN = 1024
FFN_TM = 1024
FFN_TH = 512
NORM_ROW_CHUNK = 256
ATTN_TQ = 256
DENSE_Q_TILES = 2
DIFF_TK = 1024
GQA_TK = 512
DIL_RADIUS = 64
OUT_PROJ_TM = 512


def _params(*sem):
    return pltpu.CompilerParams(dimension_semantics=sem, vmem_limit_bytes=VMEM_LIMIT_BYTES)


def _nt_dot(a, b):
    return lax.dot_general(a, b, (((1,), (1,)), ((), ())), preferred_element_type=F32)


def _rms(x, g):
    ms = jnp.mean(x * x, axis=-1, keepdims=True)
    return (x * lax.rsqrt(ms + EPS)) * g


def _norm_rows_to(x_ref, g_ref, xn_ref, rows):
    g = g_ref[...]

    def body(c, carry):
        r = pl.multiple_of(c * NORM_ROW_CHUNK, NORM_ROW_CHUNK)
        xn_ref[pl.ds(r, NORM_ROW_CHUNK), :] = _rms(x_ref[pl.ds(r, NORM_ROW_CHUNK), :], g).astype(BF16)
        return carry

    lax.fori_loop(0, rows // NORM_ROW_CHUNK, body, 0)


def _norm_proj_kernel(x_ref, g_ref, w_ref, o_ref, xn_ref, *, q_tiles):
    j = pl.program_id(1)

    @pl.when(j == 0)
    def _():
        _norm_rows_to(x_ref, g_ref, xn_ref, x_ref.shape[0])

    is_q = functools.reduce(jnp.logical_or, [j == jj for jj in q_tiles])
    factor = jnp.where(is_q, Q_PRESCALE, 1.0).astype(F32)
    acc = jnp.dot(xn_ref[...], w_ref[...], preferred_element_type=F32)
    o_ref[...] = (acc * factor).astype(o_ref.dtype)


def norm_proj(x, g, w):
    T, D = x.shape
    N = w.shape[1]
    tm, tn = PROJ_TM, PROJ_TN
    assert A_W % tn == 0 and B_W % tn == 0
    q_tiles = tuple(range(A_W // tn)) + tuple(range(3 * A_W // tn, (3 * A_W + B_W) // tn))
    return pl.pallas_call(
        functools.partial(_norm_proj_kernel, q_tiles=q_tiles),
        grid=(T // tm, N // tn),
        in_specs=[
            pl.BlockSpec((tm, D), lambda i, j: (i, 0)),
            pl.BlockSpec((1, D), lambda i, j: (0, 0)),
            pl.BlockSpec((D, tn), lambda i, j: (0, j)),
        ],
        out_specs=pl.BlockSpec((tm, tn), lambda i, j: (i, j)),
        out_shape=jax.ShapeDtypeStruct((T, N), BF16),
        scratch_shapes=[pltpu.VMEM((tm, D), BF16)],
        compiler_params=_params("parallel", "arbitrary"),
        name="norm_proj",
    )(x, g.reshape(1, D), w)


def _qk_norm_rope(x, g, cos, sin_signed):
    y = _rms(x, g)
    lane = lax.broadcasted_iota(jnp.int32, y.shape, 1)
    nxt = pltpu.roll(y, LANES - 1, 1)
    prv = pltpu.roll(y, 1, 1)
    partner = jnp.where((lane & 1) == 0, nxt, prv)
    return y * cos + partner * sin_signed


KIND_DILATED_Q, KIND_GQA_Q, KIND_GQA_K = "dilated_q", "gqa_q", "gqa_k"


def _od_head_kinds(tn):
    slots = tn // LANES
    c_hi = C_W // LANES
    q_lo, q_hi = 3 * C_W // LANES, (3 * C_W + DQ_W) // LANES
    k_hi = q_hi + DKV_W // LANES
    table = {}
    for j in range(OD_IN // tn):
        kinds = []
        for c in range(slots):
            s = j * slots + c
            kinds.append(KIND_DILATED_Q if s < c_hi else KIND_GQA_Q if q_lo <= s < q_hi
                         else KIND_GQA_K if q_hi <= s < k_hi else None)
        if any(k is not None for k in kinds):
            table[j] = kinds
    return table


def _od_proj_kernel(x_ref, g_ref, w_ref, cos_ref, sin_ref, qkg_ref, o_ref, xn_ref, *, kinds_by_tile):
    j = pl.program_id(1)

    @pl.when(j == 0)
    def _():
        _norm_rows_to(x_ref, g_ref, xn_ref, x_ref.shape[0])

    acc = jnp.dot(xn_ref[...], w_ref[...], preferred_element_type=F32)

    plain = None
    for jj, kinds in kinds_by_tile.items():
        hit = j == jj
        plain = jnp.logical_not(hit) if plain is None else jnp.logical_and(plain, jnp.logical_not(hit))

        @pl.when(hit)
        def _(kinds=kinds):
            cos = cos_ref[...]
            sin = sin_ref[...]
            for c, kind in enumerate(kinds):
                blk = acc[:, c * LANES:(c + 1) * LANES]
                if kind == KIND_GQA_Q:
                    blk = _qk_norm_rope(blk, qkg_ref[0:1, :], cos, sin) * Q_PRESCALE
                elif kind == KIND_GQA_K:
                    blk = _qk_norm_rope(blk, qkg_ref[1:2, :], cos, sin)
                elif kind == KIND_DILATED_Q:
                    blk = blk * Q_PRESCALE
                o_ref[:, c * LANES:(c + 1) * LANES] = blk.astype(o_ref.dtype)

    @pl.when(plain)
    def _():
        o_ref[...] = acc.astype(o_ref.dtype)


def od_norm_proj(x, g, w, cos, sin_signed, qk_g, S):
    T, D = x.shape
    N = w.shape[1]
    tm, tn = PROJ_TM, PROJ_TN
    s_tiles = S // tm
    kernel = functools.partial(_od_proj_kernel, kinds_by_tile=_od_head_kinds(tn))
    return pl.pallas_call(
        kernel,
        grid=(T // tm, N // tn),
        in_specs=[
            pl.BlockSpec((tm, D), lambda i, j: (i, 0)),
            pl.BlockSpec((1, D), lambda i, j: (0, 0)),
            pl.BlockSpec((D, tn), lambda i, j: (0, j)),
            pl.BlockSpec((tm, HEAD_DIM), lambda i, j: (i % s_tiles, 0)),
            pl.BlockSpec((tm, HEAD_DIM), lambda i, j: (i % s_tiles, 0)),
            pl.BlockSpec((2, HEAD_DIM), lambda i, j: (0, 0)),
        ],
        out_specs=pl.BlockSpec((tm, tn), lambda i, j: (i, j)),
        out_shape=jax.ShapeDtypeStruct((T, N), BF16),
        scratch_shapes=[pltpu.VMEM((tm, D), BF16)],
        compiler_params=_params("parallel", "arbitrary"),
        name="od_norm_proj",
    )(x, g.reshape(1, D), w, cos, sin_signed, qk_g)


def _online_softmax(qs, k_chunk, v_chunk, bias_chunk, n_chunks, row_sums):
    n = len(qs)
    m = [None] * n
    acc = [None] * n
    den = [None] * n
    for j in range(n_chunks):
        for c in range(n):
            s = _nt_dot(qs[c], k_chunk(c, j))
            bias = bias_chunk(c, j)
            if bias is not None:
                s = s + bias
            m_j = jnp.max(s, axis=-1, keepdims=True)
            m_new = m_j if j == 0 else jnp.maximum(m[c], m_j)
            p = jnp.exp2(s - m_new)
            pv = jnp.dot(p.astype(BF16), v_chunk(c, j), preferred_element_type=F32)
            if j == 0:
                acc[c] = pv
                den[c] = jnp.sum(p, axis=-1, keepdims=True) if row_sums else None
            else:
                alpha = jnp.exp2(m[c] - m_new)
                acc[c] = alpha * acc[c] + pv
                if row_sums:
                    den[c] = alpha * den[c] + jnp.sum(p, axis=-1, keepdims=True)
            m[c] = m_new
    return acc, den


def _fill_value_and_ones(v_ref, va_ref):
    rows, d = v_ref.shape
    va_ref[:, :d] = v_ref[...]
    va_ref[:, d:] = jnp.ones((rows, va_ref.shape[1] - d), va_ref.dtype)


def _diff_attn_kernel(slopes_ref, lam_ref, g_ref, q_ref, k_ref, v_ref, o_ref, bias_ref,
                      *, S, tq, n_sub, tk, lambda_init):
    h = pl.program_id(0)
    b = pl.program_id(1)
    i = pl.program_id(2)
    width = 2 * S - tq
    chunk = tq

    @pl.when(jnp.logical_and(b == 0, i == 0))
    def _():
        slope = slopes_ref[h] * LOG2E
        row = lax.broadcasted_iota(jnp.int32, (tq, chunk), 0)
        col = lax.broadcasted_iota(jnp.int32, (tq, chunk), 1)
        base = row + (S - tq) - col

        def body(c, carry):
            off = pl.multiple_of(c * chunk, chunk)
            dist = jnp.abs(base - off).astype(F32)
            bias_ref[:, pl.ds(off, chunk)] = -(slope * dist)
            return carry

        lax.fori_loop(0, width // chunk, body, 0)

    starts = [pl.multiple_of(S - tq - (i * n_sub + r) * tq, LANES) for r in range(n_sub)]
    lv = lam_ref[...]
    lam = (jnp.exp(jnp.sum(lv[0:1] * lv[1:2], axis=-1, keepdims=True))
           - jnp.exp(jnp.sum(lv[2:3] * lv[3:4], axis=-1, keepdims=True)) + lambda_init)

    qs = [q_ref[r * tq:(r + 1) * tq, c * HEAD_DIM:(c + 1) * HEAD_DIM] for r in range(n_sub) for c in range(2)]
    acc, den = _online_softmax(
        qs,
        lambda n, j: k_ref[j * tk:(j + 1) * tk, (n % 2) * HEAD_DIM:(n % 2 + 1) * HEAD_DIM],
        lambda n, j: v_ref[j * tk:(j + 1) * tk, :],
        lambda n, j: bias_ref[:, pl.ds(pl.multiple_of(starts[n // 2] + j * tk, LANES), tk)],
        S // tk, row_sums=True)
    for r in range(n_sub):
        o = acc[2 * r] / den[2 * r] - lam * (acc[2 * r + 1] / den[2 * r + 1])
        o = _rms(o, g_ref[...]) * (1.0 - lambda_init)
        o_ref[r * tq:(r + 1) * tq, :] = o.astype(o_ref.dtype)


def diff_attention(proj, lam_vec, subln_g, B, S, lambda_init):
    tq = ATTN_TQ
    dv = 2 * HEAD_DIM
    slopes = jnp.asarray([2.0 ** (-8.0 * (i + 1) / A_HEADS) for i in range(A_HEADS)], F32)
    n_sub = DENSE_Q_TILES
    tk = min(DIFF_TK, S // 4)
    kernel = functools.partial(_diff_attn_kernel, S=S, tq=tq, n_sub=n_sub, tk=tk, lambda_init=lambda_init)
    return pl.pallas_call(
        kernel,
        grid=(A_HEADS, B, S // (tq * n_sub)),
        in_specs=[
            pl.BlockSpec(memory_space=pltpu.SMEM),
            pl.BlockSpec((4, HEAD_DIM), lambda h, b, i: (0, 0)),
            pl.BlockSpec((1, dv), lambda h, b, i: (0, 0)),
            pl.BlockSpec((None, tq * n_sub, dv), lambda h, b, i: (b, i, h)),
            pl.BlockSpec((None, S, dv), lambda h, b, i: (b, 0, A_HEADS + h)),
            pl.BlockSpec((None, S, dv), lambda h, b, i: (b, 0, 2 * A_HEADS + h)),
        ],
        out_specs=pl.BlockSpec((None, tq * n_sub, dv), lambda h, b, i: (b, i, h)),
        out_shape=jax.ShapeDtypeStruct((B, S, A_W), BF16),
        scratch_shapes=[pltpu.VMEM((tq, 2 * S - tq), F32)],
        compiler_params=_params("arbitrary", "arbitrary", "arbitrary"),
        name="diff_attention",
    )(slopes, lam_vec.astype(F32), subln_g.reshape(1, dv), proj, proj, proj)


RPB_ROWS = 2 * NA_KH - 1
RPB_COLS = 2 * NA_KW - 1
NA_Q_ROWS = 4
NA_K_ROWS = 12
NA_Q = NA_Q_ROWS * GRID_W
NA_K = NA_K_ROWS * GRID_W


def _na_case_rule(case, a):
    if case == 0:
        return 0, NA_KH, NA_KH - 1 - a
    if case == 1:
        return a, a + NA_KH, NA_KH // 2 - 1 - a
    return NA_K_ROWS - NA_KH, NA_K_ROWS, -1 - a


def _nbr_attn_kernel(rpb_ref, q_ref, k_ref, v_ref, o_ref, pair_ref, tab_ref, *, S):
    h = pl.program_id(0)
    b = pl.program_id(1)
    rows = S // GRID_W
    n_groups = rows // NA_Q_ROWS

    @pl.when(b == 0)
    def _():
        c = lax.broadcasted_iota(jnp.int32, (GRID_W, LANES), 0)
        lane = lax.broadcasted_iota(jnp.int32, (GRID_W, LANES), 1)
        w = lane & (GRID_W - 1)
        upper = lane >= GRID_W
        c0 = jnp.clip(c - NA_KW // 2, 0, GRID_W - NA_KW)
        valid = jnp.logical_and(w >= c0, w < c0 + NA_KW)
        dc = jnp.clip(w - c, 1 - NA_KW, NA_KW - 1) + NA_KW - 1
        base = h * (RPB_ROWS * RPB_COLS)
        for e in range(RPB_ROWS + 1):
            def body(d, acc, e=e):
                lo = rpb_ref[base + (e - 1) * RPB_COLS + d] * LOG2E if e >= 1 else 0.0
                hi = rpb_ref[base + e * RPB_COLS + d] * LOG2E if e < RPB_ROWS else 0.0
                return jnp.where(dc == d, jnp.where(upper, hi, lo), acc)

            acc = lax.fori_loop(0, RPB_COLS, body, jnp.zeros((GRID_W, LANES), F32))
            pair_ref[e] = jnp.where(valid, acc, NEG_INF)

        for case in range(3):
            for a in range(NA_Q_ROWS):
                lo_slot, hi_slot, off = _na_case_rule(case, a)
                for t in range(NA_K_ROWS // 2):
                    see0 = lo_slot <= 2 * t < hi_slot
                    see1 = lo_slot <= 2 * t + 1 < hi_slot
                    if see0 or see1:
                        tile = pair_ref[2 * t + off + 1]
                        if not see0:
                            tile = jnp.where(upper, tile, NEG_INF)
                        if not see1:
                            tile = jnp.where(upper, NEG_INF, tile)
                    else:
                        tile = jnp.full((GRID_W, LANES), NEG_INF, F32)
                    tab_ref[case, a * GRID_W:(a + 1) * GRID_W, t * LANES:(t + 1) * LANES] = tile

    def group(gi):
        case = jnp.where(gi == 0, 0, jnp.where(gi == n_groups - 1, 2, 1))
        u0 = jnp.clip(gi * NA_Q_ROWS - NA_KH // 2, 0, rows - NA_K_ROWS)
        qs = pl.multiple_of(gi * NA_Q, NA_Q)
        ks = pl.multiple_of(u0 * GRID_W, GRID_W)
        q = q_ref[pl.ds(qs, NA_Q), :]
        k = k_ref[pl.ds(ks, NA_K), :]
        v = v_ref[pl.ds(ks, NA_K), :]
        s = _nt_dot(q, k) + tab_ref[case]
        m = jnp.max(s, axis=-1, keepdims=True)
        p = jnp.exp2(s - m)
        l = jnp.sum(p, axis=-1, keepdims=True)
        o = jnp.dot(p.astype(BF16), v, preferred_element_type=F32) / l
        o_ref[pl.ds(qs, NA_Q), :] = o.astype(o_ref.dtype)

    def step(gp, carry):
        group(2 * gp)
        group(2 * gp + 1)
        return carry

    lax.fori_loop(0, n_groups // 2, step, 0)


def neighbourhood_attention(proj, rpb, B, S):
    q0 = 3 * A_W // HEAD_DIM
    kernel = functools.partial(_nbr_attn_kernel, S=S)
    return pl.pallas_call(
        kernel,
        grid=(B_HEADS, B),
        in_specs=[
            pl.BlockSpec(memory_space=pltpu.SMEM),
            pl.BlockSpec((None, S, HEAD_DIM), lambda h, b: (b, 0, q0 + h)),
            pl.BlockSpec((None, S, HEAD_DIM), lambda h, b: (b, 0, q0 + B_HEADS + h)),
            pl.BlockSpec((None, S, HEAD_DIM), lambda h, b: (b, 0, q0 + 2 * B_HEADS + h)),
        ],
        out_specs=pl.BlockSpec((None, S, HEAD_DIM), lambda h, b: (b, 0, h)),
        out_shape=jax.ShapeDtypeStruct((B, S, B_W), BF16),
        scratch_shapes=[pltpu.VMEM((RPB_ROWS + 1, GRID_W, LANES), F32),
                        pltpu.VMEM((3, NA_Q, NA_K), F32)],
        compiler_params=_params("arbitrary", "arbitrary"),
        name="neighbourhood_attention",
    )(rpb.astype(F32).reshape(-1), proj, proj, proj)


def _dil_geometry(S, tl, dil):
    half = DIL_RADIUS * dil
    hp = -(-half // LANES) * LANES
    kw = min(S, tl + 2 * hp)
    deltas = [min(max(l0 - hp, 0), S - kw) - l0 for l0 in range(0, S, tl)]
    u0 = -min(deltas)
    return half, hp, kw, u0, kw + max(deltas) + u0


def _dilated_kernel(slopes_ref, q0_ref, q1_ref, q2_ref, k0_ref, k1_ref, k2_ref, v0_ref, v1_ref, v2_ref,
                    o_ref, b0_ref, b1_ref, b2_ref, *, S, tl, geoms):
    hh = pl.program_id(0)
    b = pl.program_id(1)
    i = pl.program_id(2)
    q_refs, k_refs, v_refs = (q0_ref, q1_ref, q2_ref), (k0_ref, k1_ref, k2_ref), (v0_ref, v1_ref, v2_ref)
    bias_refs = (b0_ref, b1_ref, b2_ref)

    @pl.when(jnp.logical_and(b == 0, i == 0))
    def _():
        for g, (_, dil) in enumerate(C_PATTERNS):
            half, _, _, u0, width = geoms[g]
            slope = slopes_ref[g * C_GROUP_HEADS + hh] * LOG2E
            row = lax.broadcasted_iota(jnp.int32, (tl, LANES), 0)
            col = lax.broadcasted_iota(jnp.int32, (tl, LANES), 1)
            base = col - row - u0

            def body(c, carry, g=g, dil=dil, half=half, slope=slope, base=base):
                off = pl.multiple_of(c * LANES, LANES)
                rel = base + off
                dist = jnp.abs(rel)
                ok = jnp.logical_and(dist <= half, (rel & (dil - 1)) == 0)
                bias_refs[g][:, pl.ds(off, LANES)] = jnp.where(ok, -(slope * dist.astype(F32)), NEG_INF)
                return carry

            lax.fori_loop(0, width // LANES, body, 0)

    l0 = i * tl
    accs, ms, ls = [], [], []
    for g in range(len(C_PATTERNS)):
        _, hp, kw, u0, _ = geoms[g]
        ws = pl.multiple_of(jnp.clip(l0 - hp, 0, S - kw), LANES)
        start = pl.multiple_of(ws - l0 + u0, LANES)
        k = k_refs[g][pl.ds(ws, kw), :]
        v = v_refs[g][pl.ds(ws, kw), :]
        s = _nt_dot(q_refs[g][...], k) + bias_refs[g][:, pl.ds(start, kw)]
        m = jnp.max(s, axis=-1, keepdims=True)
        p = jnp.exp2(s - m)
        ls.append(jnp.sum(p, axis=-1, keepdims=True))
        ms.append(m)
        accs.append(jnp.dot(p.astype(BF16), v, preferred_element_type=F32))

    m_all = jnp.maximum(jnp.maximum(ms[0], ms[1]), ms[2])
    ws_ = [jnp.exp2(m - m_all) for m in ms]
    num = ws_[0] * accs[0] + ws_[1] * accs[1] + ws_[2] * accs[2]
    den = ws_[0] * ls[0] + ws_[1] * ls[1] + ws_[2] * ls[2]
    o_ref[...] = (num / den).astype(o_ref.dtype)


def dilated_mixture_attention(proj, B, S):
    tl = ATTN_TQ
    ng = len(C_PATTERNS)
    for win, dil in C_PATTERNS:
        assert win // (2 * dil) == DIL_RADIUS and dil & (dil - 1) == 0
    geoms = tuple(_dil_geometry(S, tl, dil) for _, dil in C_PATTERNS)
    slopes = jnp.asarray([2.0 ** (-8.0 * (i + 1) / C_HEADS) for i in range(C_HEADS)], F32)
    k0, v0 = C_W // HEAD_DIM, 2 * C_W // HEAD_DIM

    def q_spec(g):
        return pl.BlockSpec((None, tl, HEAD_DIM), lambda hh, b, i: (b, i, g * C_GROUP_HEADS + hh))

    def kv_spec(first, g):
        return pl.BlockSpec((None, S, HEAD_DIM), lambda hh, b, i: (b, 0, first + g * C_GROUP_HEADS + hh))

    kernel = functools.partial(_dilated_kernel, S=S, tl=tl, geoms=geoms)
    return pl.pallas_call(
        kernel,
        grid=(C_GROUP_HEADS, B, S // tl),
        in_specs=([pl.BlockSpec(memory_space=pltpu.SMEM)]
                  + [q_spec(g) for g in range(ng)]
                  + [kv_spec(k0, g) for g in range(ng)]
                  + [kv_spec(v0, g) for g in range(ng)]),
        out_specs=pl.BlockSpec((None, tl, HEAD_DIM), lambda hh, b, i: (b, i, hh)),
        out_shape=jax.ShapeDtypeStruct((B, S, OC_W), BF16),
        scratch_shapes=[pltpu.VMEM((tl, geoms[g][4]), F32) for g in range(ng)],
        compiler_params=_params("arbitrary", "arbitrary", "arbitrary"),
        name="dilated_mixture_attention",
    )(slopes, *([proj] * (3 * ng)))


def _gqa_kernel(q_ref, k_ref, v_ref, o_ref, va_ref, *, S, tq, n_sub, tk):
    d = HEAD_DIM

    @pl.when(pl.program_id(2) == 0)
    def _():
        _fill_value_and_ones(v_ref, va_ref)

    chains = [(r, g) for r in range(n_sub) for g in range(D_GROUP)]
    qs = [q_ref[r * tq:(r + 1) * tq, g * d:(g + 1) * d] for r, g in chains]
    acc, _ = _online_softmax(
        qs,
        lambda n, j: k_ref[j * tk:(j + 1) * tk, :],
        lambda n, j: va_ref[j * tk:(j + 1) * tk, :],
        lambda n, j: None,
        S // tk, row_sums=False)
    for n, (r, g) in enumerate(chains):
        o_ref[r * tq:(r + 1) * tq, g * d:(g + 1) * d] = (acc[n][:, :d] / acc[n][:, d:]).astype(o_ref.dtype)


def gqa_attention(proj, B, S):
    tq = ATTN_TQ
    qw = D_GROUP * HEAD_DIM
    q0 = 3 * C_W // qw
    k0 = (3 * C_W + DQ_W) // HEAD_DIM
    v0 = k0 + D_KV_HEADS
    n_sub = DENSE_Q_TILES
    tk = min(GQA_TK, S)
    return pl.pallas_call(
        functools.partial(_gqa_kernel, S=S, tq=tq, n_sub=n_sub, tk=tk),
        grid=(D_KV_HEADS, B, S // (tq * n_sub)),
        in_specs=[
            pl.BlockSpec((None, tq * n_sub, qw), lambda h, b, i: (b, i, q0 + h)),
            pl.BlockSpec((None, S, HEAD_DIM), lambda h, b, i: (b, 0, k0 + h)),
            pl.BlockSpec((None, S, HEAD_DIM), lambda h, b, i: (b, 0, v0 + h)),
        ],
        out_specs=pl.BlockSpec((None, tq * n_sub, qw), lambda h, b, i: (b, i, h)),
        out_shape=jax.ShapeDtypeStruct((B, S, DQ_W), BF16),
        scratch_shapes=[pltpu.VMEM((S, 2 * HEAD_DIM), BF16)],
        compiler_params=_params("arbitrary", "arbitrary", "arbitrary"),
        name="gqa_attention",
    )(proj, proj, proj)


def _out_proj_kernel(a1_ref, a2_ref, w1_ref, w2_ref, h_ref, o_ref):
    acc = jnp.dot(a1_ref[...], w1_ref[...], preferred_element_type=F32)
    acc = acc + jnp.dot(a2_ref[...], w2_ref[...], preferred_element_type=F32)
    o_ref[...] = h_ref[...] + acc


def out_proj_residual(a1, a2, w1, w2, h):
    T, D = h.shape
    k1, k2 = a1.shape[1], a2.shape[1]
    tm = OUT_PROJ_TM
    resident = pl.Buffered(1)
    return pl.pallas_call(
        _out_proj_kernel,
        grid=(T // tm,),
        in_specs=[
            pl.BlockSpec((tm, k1), lambda i: (i, 0)),
            pl.BlockSpec((tm, k2), lambda i: (i, 0)),
            pl.BlockSpec((k1, D), lambda i: (0, 0), pipeline_mode=resident),
            pl.BlockSpec((k2, D), lambda i: (0, 0), pipeline_mode=resident),
            pl.BlockSpec((tm, D), lambda i: (i, 0)),
        ],
        out_specs=pl.BlockSpec((tm, D), lambda i: (i, 0)),
        out_shape=jax.ShapeDtypeStruct((T, D), F32),
        compiler_params=_params("parallel"),
        name="out_proj_residual",
    )(a1, a2, w1, w2, h)


def _ffn_kernel(x_ref, g_ref, wg_ref, wu_ref, wd_ref, fg_ref, o_ref, xn_ref, *, final_norm):
    k = pl.program_id(1)

    @pl.when(k == 0)
    def _():
        _norm_rows_to(x_ref, g_ref, xn_ref, x_ref.shape[0])
        o_ref[...] = x_ref[...]

    xn = xn_ref[...]
    gate = jnp.dot(xn, wg_ref[...], preferred_element_type=F32)
    up = jnp.dot(xn, wu_ref[...], preferred_element_type=F32)
    act = (gate * jax.nn.sigmoid(gate)) * up
    o_ref[...] += jnp.dot(act.astype(BF16), wd_ref[...], preferred_element_type=F32)

    if final_norm:
        @pl.when(k == pl.num_programs(1) - 1)
        def _():
            fg = fg_ref[...]

            def body(c, carry):
                r = pl.multiple_of(c * NORM_ROW_CHUNK, NORM_ROW_CHUNK)
                o_ref[pl.ds(r, NORM_ROW_CHUNK), :] = _rms(o_ref[pl.ds(r, NORM_ROW_CHUNK), :], fg)
                return carry

            lax.fori_loop(0, o_ref.shape[0] // NORM_ROW_CHUNK, body, 0)


def ffn_residual(x, g, wg, wu, wd, final_g, final_norm):
    T, D = x.shape
    H = wg.shape[1]
    tm, th = FFN_TM, FFN_TH
    kernel = functools.partial(_ffn_kernel, final_norm=final_norm)
    return pl.pallas_call(
        kernel,
        grid=(T // tm, H // th),
        in_specs=[
            pl.BlockSpec((tm, D), lambda i, k: (i, 0)),
            pl.BlockSpec((1, D), lambda i, k: (0, 0)),
            pl.BlockSpec((D, th), lambda i, k: (0, k)),
            pl.BlockSpec((D, th), lambda i, k: (0, k)),
            pl.BlockSpec((th, D), lambda i, k: (k, 0)),
            pl.BlockSpec((1, D), lambda i, k: (0, 0)),
        ],
        out_specs=pl.BlockSpec((tm, D), lambda i, k: (i, 0)),
        out_shape=jax.ShapeDtypeStruct((T, D), F32),
        scratch_shapes=[pltpu.VMEM((tm, D), BF16)],
        compiler_params=_params("parallel", "arbitrary"),
        name="ffn_residual",
    )(x, g.reshape(1, D), wg, wu, wd, final_g.reshape(1, D))


def _rope_tables(S):
    t = jnp.arange(S)
    row = (t // GRID_W).astype(F32)
    col = (t % GRID_W).astype(F32)
    half = HEAD_DIM // 2
    f_row = ROPE_THETA ** (-jnp.arange(0, half, 2, dtype=F32) / half)
    f_col = ROPE_THETA ** (-jnp.arange(0, HEAD_DIM - half, 2, dtype=F32) / (HEAD_DIM - half))
    ang = jnp.concatenate([row[:, None] * f_row[None, :], col[:, None] * f_col[None, :]], axis=-1)
    cos = jnp.repeat(jnp.cos(ang), 2, axis=-1)
    sin = jnp.repeat(jnp.sin(ang), 2, axis=-1)
    sign = jnp.where(jnp.arange(HEAD_DIM) % 2 == 0, -1.0, 1.0).astype(F32)
    return cos, sin * sign[None, :]


def _lambda_init(layer_idx):
    return 0.8 - 0.6 * math.exp(-0.3 * layer_idx)


def _trunk(x, w):
    B, S, D = x.shape
    T = B * S
    h = x.reshape(T, D)
    cos, sin_signed = _rope_tables(S)
    for i in range(DEPTH):
        j = i // 2
        if i % 2 == 0:
            proj = norm_proj(h, w["attn_norm_g"][i], w["ev_w_in"][j])
            proj3 = proj.reshape(B, S, EV_IN)
            oa = diff_attention(proj3, w["ev_lambda"][j], w["ev_subln_g"][j], B, S, _lambda_init(i))
            ob = neighbourhood_attention(proj3, w["ev_rpb"][j], B, S)
            h = out_proj_residual(oa.reshape(T, A_W), ob.reshape(T, B_W),
                                  w["ev_w_out"][j][0], w["ev_w_out"][j][1], h)
        else:
            proj = od_norm_proj(h, w["attn_norm_g"][i], w["od_w_in"][j], cos, sin_signed,
                                w["od_qk_norm_g"][j].astype(F32), S)
            proj3 = proj.reshape(B, S, OD_IN)
            oc = dilated_mixture_attention(proj3, B, S).reshape(T, OC_W)
            od = gqa_attention(proj3, B, S)
            h = out_proj_residual(oc, od.reshape(T, DQ_W), w["od_w_out"][j][0], w["od_w_out"][j][1], h)
        h = ffn_residual(h, w["ffn_norm_g"][i], w["ffn_w_gate"][i], w["ffn_w_up"][i], w["ffn_w_down"][i],
                         w["final_norm_g"], final_norm=(i == DEPTH - 1))
    return h.reshape(B, S, D)


def kernel(x_prompt, x_sample, attn_norm_g, ev_w_in, ev_lambda, ev_subln_g, ev_rpb, ev_w_out, od_w_in,
           od_qk_norm_g, od_w_out, ffn_norm_g, ffn_w_gate, ffn_w_up, ffn_w_down, final_norm_g):
    w = {
        "attn_norm_g": attn_norm_g, "ffn_norm_g": ffn_norm_g, "final_norm_g": final_norm_g,
        "ev_lambda": ev_lambda, "ev_subln_g": ev_subln_g, "ev_rpb": ev_rpb, "od_qk_norm_g": od_qk_norm_g,
        "ev_w_in": [ev_w_in[j].astype(BF16) for j in range(ev_w_in.shape[0])],
        "ev_w_out": [(ev_w_out[j, :A_W].astype(BF16), ev_w_out[j, A_W:].astype(BF16))
                     for j in range(ev_w_out.shape[0])],
        "od_w_in": [od_w_in[j].astype(BF16) for j in range(od_w_in.shape[0])],
        "od_w_out": [(od_w_out[j, :OC_W].astype(BF16), od_w_out[j, OC_W:].astype(BF16))
                     for j in range(od_w_out.shape[0])],
        "ffn_w_gate": [ffn_w_gate[i].astype(BF16) for i in range(DEPTH)],
        "ffn_w_up": [ffn_w_up[i].astype(BF16) for i in range(DEPTH)],
        "ffn_w_down": [ffn_w_down[i].astype(BF16) for i in range(DEPTH)],
    }
    return _trunk(x_prompt, w), _trunk(x_sample, w)
```

```python
import functools
import math

import jax
import jax.numpy as jnp
from jax import lax
from jax.experimental import pallas as pl
from jax.experimental.pallas import tpu as pltpu

F32 = jnp.float32
BF16 = jnp.bfloat16

D_MODEL = 2048
DEPTH = 2
HEAD_DIM = 128
GRID_W = 64
EPS = 1e-6
NEG_INF = -1e30
SCALE = HEAD_DIM ** -0.5
LOG2E = math.log2(math.e)
Q_PRESCALE = SCALE * LOG2E

A_HEADS = 4
B_HEADS = 8
NA_KH = 8
NA_KW = 16
C_PATTERNS = ((128, 1), (512, 4), (2048, 16))
C_GROUP_HEADS = 4
C_HEADS = 12
D_HEADS = 12
D_KV_HEADS = 4
D_GROUP = D_HEADS // D_KV_HEADS
ROPE_THETA = 10000.0
FFN_HIDDEN = 5632

A_W = A_HEADS * 2 * HEAD_DIM
B_W = B_HEADS * HEAD_DIM
EV_IN = 3 * A_W + 3 * B_W
C_W = C_HEADS * HEAD_DIM
DQ_W = D_HEADS * HEAD_DIM
DKV_W = D_KV_HEADS * HEAD_DIM
OD_IN = 3 * C_W + DQ_W + 2 * DKV_W
OC_W = C_GROUP_HEADS * HEAD_DIM

VMEM_LIMIT_BYTES = 56 * 1024 * 1024
LANES = 128

PROJ_TM = 1024
PROJ_TN = 1024
FFN_TM = 1024
FFN_TH = 512
NORM_ROW_CHUNK = 256
ATTN_TQ = 256
DENSE_Q_TILES = 4
DIFF_TK = 1024
GQA_TK = 512
DIL_RADIUS = 64
OUT_PROJ_TM = 512


def _params(*sem):
    return pltpu.CompilerParams(dimension_semantics=sem, vmem_limit_bytes=VMEM_LIMIT_BYTES)


def _nt_dot(a, b):
    return lax.dot_general(a, b, (((1,), (1,)), ((), ())), preferred_element_type=F32)


def _rms(x, g):
    ms = jnp.mean(x * x, axis=-1, keepdims=True)
    return (x * lax.rsqrt(ms + EPS)) * g


def _norm_rows_to(x_ref, g_ref, xn_ref, rows):
    g = g_ref[...]

    def body(c, carry):
        r = pl.multiple_of(c * NORM_ROW_CHUNK, NORM_ROW_CHUNK)
        xn_ref[pl.ds(r, NORM_ROW_CHUNK), :] = _rms(x_ref[pl.ds(r, NORM_ROW_CHUNK), :], g).astype(BF16)
        return carry

    lax.fori_loop(0, rows // NORM_ROW_CHUNK, body, 0)


def _norm_proj_kernel(x_ref, g_ref, w_ref, o_ref, xn_ref, *, q_tiles):
    j = pl.program_id(1)

    @pl.when(j == 0)
    def _():
        _norm_rows_to(x_ref, g_ref, xn_ref, x_ref.shape[0])

    is_q = functools.reduce(jnp.logical_or, [j == jj for jj in q_tiles])
    factor = jnp.where(is_q, Q_PRESCALE, 1.0).astype(F32)
    acc = jnp.dot(xn_ref[...], w_ref[...], preferred_element_type=F32)
    o_ref[...] = (acc * factor).astype(o_ref.dtype)


def norm_proj(x, g, w):
    T, D = x.shape
    N = w.shape[1]
    tm, tn = PROJ_TM, PROJ_TN
    assert A_W % tn == 0 and B_W % tn == 0
    q_tiles = tuple(range(A_W // tn)) + tuple(range(3 * A_W // tn, (3 * A_W + B_W) // tn))
    return pl.pallas_call(
        functools.partial(_norm_proj_kernel, q_tiles=q_tiles),
        grid=(T // tm, N // tn),
        in_specs=[
            pl.BlockSpec((tm, D), lambda i, j: (i, 0)),
            pl.BlockSpec((1, D), lambda i, j: (0, 0)),
            pl.BlockSpec((D, tn), lambda i, j: (0, j)),
        ],
        out_specs=pl.BlockSpec((tm, tn), lambda i, j: (i, j)),
        out_shape=jax.ShapeDtypeStruct((T, N), BF16),
        scratch_shapes=[pltpu.VMEM((tm, D), BF16)],
        compiler_params=_params("parallel", "arbitrary"),
        name="norm_proj",
    )(x, g.reshape(1, D), w)


def _qk_norm_rope(x, g, cos, sin_signed):
    y = _rms(x, g)
    lane = lax.broadcasted_iota(jnp.int32, y.shape, 1)
    nxt = pltpu.roll(y, LANES - 1, 1)
    prv = pltpu.roll(y, 1, 1)
    partner = jnp.where((lane & 1) == 0, nxt, prv)
    return y * cos + partner * sin_signed


KIND_DILATED_Q, KIND_GQA_Q, KIND_GQA_K = "dilated_q", "gqa_q", "gqa_k"


def _od_head_kinds(tn):
    slots = tn // LANES
    c_hi = C_W // LANES
    q_lo, q_hi = 3 * C_W // LANES, (3 * C_W + DQ_W) // LANES
    k_hi = q_hi + DKV_W // LANES
    table = {}
    for j in range(OD_IN // tn):
        kinds = []
        for c in range(slots):
            s = j * slots + c
            kinds.append(KIND_DILATED_Q if s < c_hi else KIND_GQA_Q if q_lo <= s < q_hi
                         else KIND_GQA_K if q_hi <= s < k_hi else None)
        if any(k is not None for k in kinds):
            table[j] = kinds
    return table


def _od_proj_kernel(x_ref, g_ref, w_ref, cos_ref, sin_ref, qkg_ref, o_ref, xn_ref, *, kinds_by_tile):
    j = pl.program_id(1)

    @pl.when(j == 0)
    def _():
        _norm_rows_to(x_ref, g_ref, xn_ref, x_ref.shape[0])

    acc = jnp.dot(xn_ref[...], w_ref[...], preferred_element_type=F32)

    plain = None
    for jj, kinds in kinds_by_tile.items():
        hit = j == jj
        plain = jnp.logical_not(hit) if plain is None else jnp.logical_and(plain, jnp.logical_not(hit))

        @pl.when(hit)
        def _(kinds=kinds):
            cos = cos_ref[...]
            sin = sin_ref[...]
            for c, kind in enumerate(kinds):
                blk = acc[:, c * LANES:(c + 1) * LANES]
                if kind == KIND_GQA_Q:
                    blk = _qk_norm_rope(blk, qkg_ref[0:1, :], cos, sin) * Q_PRESCALE
                elif kind == KIND_GQA_K:
                    blk = _qk_norm_rope(blk, qkg_ref[1:2, :], cos, sin)
                elif kind == KIND_DILATED_Q:
                    blk = blk * Q_PRESCALE
                o_ref[:, c * LANES:(c + 1) * LANES] = blk.astype(o_ref.dtype)

    @pl.when(plain)
    def _():
        o_ref[...] = acc.astype(o_ref.dtype)


def od_norm_proj(x, g, w, cos, sin_signed, qk_g, S):
    T, D = x.shape
    N = w.shape[1]
    tm, tn = PROJ_TM, PROJ_TN
    s_tiles = S // tm
    kernel = functools.partial(_od_proj_kernel, kinds_by_tile=_od_head_kinds(tn))
    return pl.pallas_call(
        kernel,
        grid=(T // tm, N // tn),
        in_specs=[
            pl.BlockSpec((tm, D), lambda i, j: (i, 0)),
            pl.BlockSpec((1, D), lambda i, j: (0, 0)),
            pl.BlockSpec((D, tn), lambda i, j: (0, j)),
            pl.BlockSpec((tm, HEAD_DIM), lambda i, j: (i % s_tiles, 0)),
            pl.BlockSpec((tm, HEAD_DIM), lambda i, j: (i % s_tiles, 0)),
            pl.BlockSpec((2, HEAD_DIM), lambda i, j: (0, 0)),
        ],
        out_specs=pl.BlockSpec((tm, tn), lambda i, j: (i, j)),
        out_shape=jax.ShapeDtypeStruct((T, N), BF16),
        scratch_shapes=[pltpu.VMEM((tm, D), BF16)],
        compiler_params=_params("parallel", "arbitrary"),
        name="od_norm_proj",
    )(x, g.reshape(1, D), w, cos, sin_signed, qk_g)


def _online_softmax(qs, k_chunk, v_chunk, bias_chunk, n_chunks, row_sums):
    n = len(qs)
    m = [None] * n
    acc = [None] * n
    den = [None] * n
    for j in range(n_chunks):
        for c in range(n):
            s = _nt_dot(qs[c], k_chunk(c, j))
            bias = bias_chunk(c, j)
            if bias is not None:
                s = s + bias
            m_j = jnp.max(s, axis=-1, keepdims=True)
            m_new = m_j if j == 0 else jnp.maximum(m[c], m_j)
            p = jnp.exp2(s - m_new)
            pv = jnp.dot(p.astype(BF16), v_chunk(c, j), preferred_element_type=F32)
            if j == 0:
                acc[c] = pv
                den[c] = jnp.sum(p, axis=-1, keepdims=True) if row_sums else None
            else:
                alpha = jnp.exp2(m[c] - m_new)
                acc[c] = alpha * acc[c] + pv
                if row_sums:
                    den[c] = alpha * den[c] + jnp.sum(p, axis=-1, keepdims=True)
            m[c] = m_new
    return acc, den


def _fill_value_and_ones(v_ref, va_ref):
    rows, d = v_ref.shape
    va_ref[:, :d] = v_ref[...]
    va_ref[:, d:] = jnp.ones((rows, va_ref.shape[1] - d), va_ref.dtype)


def _diff_attn_kernel(slopes_ref, lam_ref, g_ref, q_ref, k_ref, v_ref, o_ref, bias_ref,
                      *, S, tq, n_sub, tk, lambda_init):
    h = pl.program_id(0)
    b = pl.program_id(1)
    i = pl.program_id(2)
    width = 2 * S - tq
    chunk = tq

    @pl.when(jnp.logical_and(b == 0, i == 0))
    def _():
        slope = slopes_ref[h] * LOG2E
        row = lax.broadcasted_iota(jnp.int32, (tq, chunk), 0)
        col = lax.broadcasted_iota(jnp.int32, (tq, chunk), 1)
        base = row + (S - tq) - col

        def body(c, carry):
            off = pl.multiple_of(c * chunk, chunk)
            dist = jnp.abs(base - off).astype(F32)
            bias_ref[:, pl.ds(off, chunk)] = -(slope * dist)
            return carry

        lax.fori_loop(0, width // chunk, body, 0)

    starts = [pl.multiple_of(S - tq - (i * n_sub + r) * tq, LANES) for r in range(n_sub)]
    lv = lam_ref[...]
    lam = (jnp.exp(jnp.sum(lv[0:1] * lv[1:2], axis=-1, keepdims=True))
           - jnp.exp(jnp.sum(lv[2:3] * lv[3:4], axis=-1, keepdims=True)) + lambda_init)

    qs = [q_ref[r * tq:(r + 1) * tq, c * HEAD_DIM:(c + 1) * HEAD_DIM] for r in range(n_sub) for c in range(2)]
    acc, den = _online_softmax(
        qs,
        lambda n, j: k_ref[j * tk:(j + 1) * tk, (n % 2) * HEAD_DIM:(n % 2 + 1) * HEAD_DIM],
        lambda n, j: v_ref[j * tk:(j + 1) * tk, :],
        lambda n, j: bias_ref[:, pl.ds(pl.multiple_of(starts[n // 2] + j * tk, LANES), tk)],
        S // tk, row_sums=True)
    for r in range(n_sub):
        o = acc[2 * r] / den[2 * r] - lam * (acc[2 * r + 1] / den[2 * r + 1])
        o = _rms(o, g_ref[...]) * (1.0 - lambda_init)
        o_ref[r * tq:(r + 1) * tq, :] = o.astype(o_ref.dtype)


def diff_attention(proj, lam_vec, subln_g, B, S, lambda_init):
    tq = ATTN_TQ
    dv = 2 * HEAD_DIM
    slopes = jnp.asarray([2.0 ** (-8.0 * (i + 1) / A_HEADS) for i in range(A_HEADS)], F32)
    n_sub = DENSE_Q_TILES
    tk = min(DIFF_TK, S // 4)
    kernel = functools.partial(_diff_attn_kernel, S=S, tq=tq, n_sub=n_sub, tk=tk, lambda_init=lambda_init)
    return pl.pallas_call(
        kernel,
        grid=(A_HEADS, B, S // (tq * n_sub)),
        in_specs=[
            pl.BlockSpec(memory_space=pltpu.SMEM),
            pl.BlockSpec((4, HEAD_DIM), lambda h, b, i: (0, 0)),
            pl.BlockSpec((1, dv), lambda h, b, i: (0, 0)),
            pl.BlockSpec((None, tq * n_sub, dv), lambda h, b, i: (b, i, h)),
            pl.BlockSpec((None, S, dv), lambda h, b, i: (b, 0, A_HEADS + h)),
            pl.BlockSpec((None, S, dv), lambda h, b, i: (b, 0, 2 * A_HEADS + h)),
        ],
        out_specs=pl.BlockSpec((None, tq * n_sub, dv), lambda h, b, i: (b, i, h)),
        out_shape=jax.ShapeDtypeStruct((B, S, A_W), BF16),
        scratch_shapes=[pltpu.VMEM((tq, 2 * S - tq), F32)],
        compiler_params=_params("arbitrary", "arbitrary", "arbitrary"),
        name="diff_attention",
    )(slopes, lam_vec.astype(F32), subln_g.reshape(1, dv), proj, proj, proj)


RPB_ROWS = 2 * NA_KH - 1
RPB_COLS = 2 * NA_KW - 1
NA_Q_ROWS = 4
NA_K_ROWS = 12
NA_Q = NA_Q_ROWS * GRID_W
NA_K = NA_K_ROWS * GRID_W
NA_GROUPS_PER_STEP = 4


def _na_case_rule(case, a):
    if case == 0:
        return 0, NA_KH, NA_KH - 1 - a
    if case == 1:
        return a, a + NA_KH, NA_KH // 2 - 1 - a
    return NA_K_ROWS - NA_KH, NA_K_ROWS, -1 - a


def _nbr_attn_kernel(rpb_ref, q_ref, k_ref, v_ref, o_ref, pair_ref, tab_ref, *, S):
    h = pl.program_id(0)
    b = pl.program_id(1)
    rows = S // GRID_W
    n_groups = rows // NA_Q_ROWS

    @pl.when(b == 0)
    def _():
        c = lax.broadcasted_iota(jnp.int32, (GRID_W, LANES), 0)
        lane = lax.broadcasted_iota(jnp.int32, (GRID_W, LANES), 1)
        w = lane & (GRID_W - 1)
        upper = lane >= GRID_W
        c0 = jnp.clip(c - NA_KW // 2, 0, GRID_W - NA_KW)
        valid = jnp.logical_and(w >= c0, w < c0 + NA_KW)
        dc = jnp.clip(w - c, 1 - NA_KW, NA_KW - 1) + NA_KW - 1
        base = h * (RPB_ROWS * RPB_COLS)
        for e in range(RPB_ROWS + 1):
            def body(d, acc, e=e):
                lo = rpb_ref[base + (e - 1) * RPB_COLS + d] * LOG2E if e >= 1 else 0.0
                hi = rpb_ref[base + e * RPB_COLS + d] * LOG2E if e < RPB_ROWS else 0.0
                return jnp.where(dc == d, jnp.where(upper, hi, lo), acc)

            acc = lax.fori_loop(0, RPB_COLS, body, jnp.zeros((GRID_W, LANES), F32))
            pair_ref[e] = jnp.where(valid, acc, NEG_INF)

        for case in range(3):
            for a in range(NA_Q_ROWS):
                lo_slot, hi_slot, off = _na_case_rule(case, a)
                for t in range(NA_K_ROWS // 2):
                    see0 = lo_slot <= 2 * t < hi_slot
                    see1 = lo_slot <= 2 * t + 1 < hi_slot
                    if see0 or see1:
                        tile = pair_ref[2 * t + off + 1]
                        if not see0:
                            tile = jnp.where(upper, tile, NEG_INF)
                        if not see1:
                            tile = jnp.where(upper, NEG_INF, tile)
                    else:
                        tile = jnp.full((GRID_W, LANES), NEG_INF, F32)
                    tab_ref[case, a * GRID_W:(a + 1) * GRID_W, t * LANES:(t + 1) * LANES] = tile

    def group(gi):
        case = jnp.where(gi == 0, 0, jnp.where(gi == n_groups - 1, 2, 1))
        u0 = jnp.clip(gi * NA_Q_ROWS - NA_KH // 2, 0, rows - NA_K_ROWS)
        qs = pl.multiple_of(gi * NA_Q, NA_Q)
        ks = pl.multiple_of(u0 * GRID_W, GRID_W)
        q = q_ref[pl.ds(qs, NA_Q), :]
        k = k_ref[pl.ds(ks, NA_K), :]
        v = v_ref[pl.ds(ks, NA_K), :]
        s = _nt_dot(q, k) + tab_ref[case]
        m = jnp.max(s, axis=-1, keepdims=True)
        p = jnp.exp2(s - m)
        l = jnp.sum(p, axis=-1, keepdims=True)
        o = jnp.dot(p.astype(BF16), v, preferred_element_type=F32) / l
        o_ref[pl.ds(qs, NA_Q), :] = o.astype(o_ref.dtype)

    def step(gp, carry):
        for a in range(NA_GROUPS_PER_STEP):
            group(NA_GROUPS_PER_STEP * gp + a)
        return carry

    lax.fori_loop(0, n_groups // NA_GROUPS_PER_STEP, step, 0)


def neighbourhood_attention(proj, rpb, B, S):
    q0 = 3 * A_W // HEAD_DIM
    kernel = functools.partial(_nbr_attn_kernel, S=S)
    return pl.pallas_call(
        kernel,
        grid=(B_HEADS, B),
        in_specs=[
            pl.BlockSpec(memory_space=pltpu.SMEM),
            pl.BlockSpec((None, S, HEAD_DIM), lambda h, b: (b, 0, q0 + h)),
            pl.BlockSpec((None, S, HEAD_DIM), lambda h, b: (b, 0, q0 + B_HEADS + h)),
            pl.BlockSpec((None, S, HEAD_DIM), lambda h, b: (b, 0, q0 + 2 * B_HEADS + h)),
        ],
        out_specs=pl.BlockSpec((None, S, HEAD_DIM), lambda h, b: (b, 0, h)),
        out_shape=jax.ShapeDtypeStruct((B, S, B_W), BF16),
        scratch_shapes=[pltpu.VMEM((RPB_ROWS + 1, GRID_W, LANES), F32),
                        pltpu.VMEM((3, NA_Q, NA_K), F32)],
        compiler_params=_params("arbitrary", "arbitrary"),
        name="neighbourhood_attention",
    )(rpb.astype(F32).reshape(-1), proj, proj, proj)


def _dil_geometry(S, tl, dil):
    half = DIL_RADIUS * dil
    hp = -(-half // LANES) * LANES
    kw = min(S, tl + 2 * hp)
    deltas = [min(max(l0 - hp, 0), S - kw) - l0 for l0 in range(0, S, tl)]
    u0 = -min(deltas)
    return half, hp, kw, u0, kw + max(deltas) + u0


def _dilated_kernel(slopes_ref, q0_ref, q1_ref, q2_ref, k0_ref, k1_ref, k2_ref, v0_ref, v1_ref, v2_ref,
                    o_ref, b0_ref, b1_ref, b2_ref, *, S, tl, n_sub, geoms):
    hh = pl.program_id(0)
    b = pl.program_id(1)
    i = pl.program_id(2)
    q_refs, k_refs, v_refs = (q0_ref, q1_ref, q2_ref), (k0_ref, k1_ref, k2_ref), (v0_ref, v1_ref, v2_ref)
    bias_refs = (b0_ref, b1_ref, b2_ref)

    @pl.when(jnp.logical_and(b == 0, i == 0))
    def _():
        for g, (_, dil) in enumerate(C_PATTERNS):
            half, _, _, u0, width = geoms[g]
            slope = slopes_ref[g * C_GROUP_HEADS + hh] * LOG2E
            row = lax.broadcasted_iota(jnp.int32, (tl, LANES), 0)
            col = lax.broadcasted_iota(jnp.int32, (tl, LANES), 1)
            base = col - row - u0

            def body(c, carry, g=g, dil=dil, half=half, slope=slope, base=base):
                off = pl.multiple_of(c * LANES, LANES)
                rel = base + off
                dist = jnp.abs(rel)
                ok = jnp.logical_and(dist <= half, (rel & (dil - 1)) == 0)
                bias_refs[g][:, pl.ds(off, LANES)] = jnp.where(ok, -(slope * dist.astype(F32)), NEG_INF)
                return carry

            lax.fori_loop(0, width // LANES, body, 0)

    for r in range(n_sub):
        rows = slice(r * tl, (r + 1) * tl)
        l0 = (i * n_sub + r) * tl
        accs, ms, ls = [], [], []
        for g in range(len(C_PATTERNS)):
            _, hp, kw, u0, _ = geoms[g]
            ws = pl.multiple_of(jnp.clip(l0 - hp, 0, S - kw), LANES)
            start = pl.multiple_of(ws - l0 + u0, LANES)
            k = k_refs[g][pl.ds(ws, kw), :]
            v = v_refs[g][pl.ds(ws, kw), :]
            s = _nt_dot(q_refs[g][rows, :], k) + bias_refs[g][:, pl.ds(start, kw)]
            m = jnp.max(s, axis=-1, keepdims=True)
            p = jnp.exp2(s - m)
            ls.append(jnp.sum(p, axis=-1, keepdims=True))
            ms.append(m)
            accs.append(jnp.dot(p.astype(BF16), v, preferred_element_type=F32))

        m_all = jnp.maximum(jnp.maximum(ms[0], ms[1]), ms[2])
        ws_ = [jnp.exp2(m - m_all) for m in ms]
        num = ws_[0] * accs[0] + ws_[1] * accs[1] + ws_[2] * accs[2]
        den = ws_[0] * ls[0] + ws_[1] * ls[1] + ws_[2] * ls[2]
        o_ref[rows, :] = (num / den).astype(o_ref.dtype)


def dilated_mixture_attention(proj, B, S):
    tl = ATTN_TQ
    ng = len(C_PATTERNS)
    for win, dil in C_PATTERNS:
        assert win // (2 * dil) == DIL_RADIUS and dil & (dil - 1) == 0
    geoms = tuple(_dil_geometry(S, tl, dil) for _, dil in C_PATTERNS)
    slopes = jnp.asarray([2.0 ** (-8.0 * (i + 1) / C_HEADS) for i in range(C_HEADS)], F32)
    k0, v0 = C_W // HEAD_DIM, 2 * C_W // HEAD_DIM

    n_sub = DENSE_Q_TILES

    def q_spec(g):
        return pl.BlockSpec((None, tl * n_sub, HEAD_DIM), lambda hh, b, i: (b, i, g * C_GROUP_HEADS + hh))

    def kv_spec(first, g):
        return pl.BlockSpec((None, S, HEAD_DIM), lambda hh, b, i: (b, 0, first + g * C_GROUP_HEADS + hh))

    kernel = functools.partial(_dilated_kernel, S=S, tl=tl, n_sub=n_sub, geoms=geoms)
    return pl.pallas_call(
        kernel,
        grid=(C_GROUP_HEADS, B, S // (tl * n_sub)),
        in_specs=([pl.BlockSpec(memory_space=pltpu.SMEM)]
                  + [q_spec(g) for g in range(ng)]
                  + [kv_spec(k0, g) for g in range(ng)]
                  + [kv_spec(v0, g) for g in range(ng)]),
        out_specs=pl.BlockSpec((None, tl * n_sub, HEAD_DIM), lambda hh, b, i: (b, i, hh)),
        out_shape=jax.ShapeDtypeStruct((B, S, OC_W), BF16),
        scratch_shapes=[pltpu.VMEM((tl, geoms[g][4]), F32) for g in range(ng)],
        compiler_params=_params("arbitrary", "arbitrary", "arbitrary"),
        name="dilated_mixture_attention",
    )(slopes, *([proj] * (3 * ng)))


def _gqa_kernel(q_ref, k_ref, v_ref, o_ref, va_ref, *, S, tq, n_sub, tk):
    d = HEAD_DIM

    @pl.when(pl.program_id(2) == 0)
    def _():
        _fill_value_and_ones(v_ref, va_ref)

    chains = [(r, g) for r in range(n_sub) for g in range(D_GROUP)]
    qs = [q_ref[r * tq:(r + 1) * tq, g * d:(g + 1) * d] for r, g in chains]
    acc, _ = _online_softmax(
        qs,
        lambda n, j: k_ref[j * tk:(j + 1) * tk, :],
        lambda n, j: va_ref[j * tk:(j + 1) * tk, :],
        lambda n, j: None,
        S // tk, row_sums=False)
    for n, (r, g) in enumerate(chains):
        o_ref[r * tq:(r + 1) * tq, g * d:(g + 1) * d] = (acc[n][:, :d] / acc[n][:, d:]).astype(o_ref.dtype)


def gqa_attention(proj, B, S):
    tq = ATTN_TQ
    qw = D_GROUP * HEAD_DIM
    q0 = 3 * C_W // qw
    k0 = (3 * C_W + DQ_W) // HEAD_DIM
    v0 = k0 + D_KV_HEADS
    n_sub = DENSE_Q_TILES
    tk = min(GQA_TK, S)
    return pl.pallas_call(
        functools.partial(_gqa_kernel, S=S, tq=tq, n_sub=n_sub, tk=tk),
        grid=(D_KV_HEADS, B, S // (tq * n_sub)),
        in_specs=[
            pl.BlockSpec((None, tq * n_sub, qw), lambda h, b, i: (b, i, q0 + h)),
            pl.BlockSpec((None, S, HEAD_DIM), lambda h, b, i: (b, 0, k0 + h)),
            pl.BlockSpec((None, S, HEAD_DIM), lambda h, b, i: (b, 0, v0 + h)),
        ],
        out_specs=pl.BlockSpec((None, tq * n_sub, qw), lambda h, b, i: (b, i, h)),
        out_shape=jax.ShapeDtypeStruct((B, S, DQ_W), BF16),
        scratch_shapes=[pltpu.VMEM((S, 2 * HEAD_DIM), BF16)],
        compiler_params=_params("arbitrary", "arbitrary", "arbitrary"),
        name="gqa_attention",
    )(proj, proj, proj)


def _out_proj_kernel(a1_ref, a2_ref, w1_ref, w2_ref, h_ref, o_ref):
    acc = jnp.dot(a1_ref[...], w1_ref[...], preferred_element_type=F32)
    acc = acc + jnp.dot(a2_ref[...], w2_ref[...], preferred_element_type=F32)
    o_ref[...] = h_ref[...] + acc


def out_proj_residual(a1, a2, w1, w2, h):
    T, D = h.shape
    k1, k2 = a1.shape[1], a2.shape[1]
    tm = OUT_PROJ_TM
    resident = pl.Buffered(1)
    return pl.pallas_call(
        _out_proj_kernel,
        grid=(T // tm,),
        in_specs=[
            pl.BlockSpec((tm, k1), lambda i: (i, 0)),
            pl.BlockSpec((tm, k2), lambda i: (i, 0)),
            pl.BlockSpec((k1, D), lambda i: (0, 0), pipeline_mode=resident),
            pl.BlockSpec((k2, D), lambda i: (0, 0), pipeline_mode=resident),
            pl.BlockSpec((tm, D), lambda i: (i, 0)),
        ],
        out_specs=pl.BlockSpec((tm, D), lambda i: (i, 0)),
        out_shape=jax.ShapeDtypeStruct((T, D), F32),
        compiler_params=_params("parallel"),
        name="out_proj_residual",
    )(a1, a2, w1, w2, h)


def _ffn_kernel(x_ref, g_ref, wg_ref, wu_ref, wd_ref, fg_ref, o_ref, xn_ref, *, final_norm):
    k = pl.program_id(1)

    @pl.when(k == 0)
    def _():
        _norm_rows_to(x_ref, g_ref, xn_ref, x_ref.shape[0])
        o_ref[...] = x_ref[...]

    xn = xn_ref[...]
    gate = jnp.dot(xn, wg_ref[...], preferred_element_type=F32)
    up = jnp.dot(xn, wu_ref[...], preferred_element_type=F32)
    act = (gate * jax.nn.sigmoid(gate)) * up
    o_ref[...] += jnp.dot(act.astype(BF16), wd_ref[...], preferred_element_type=F32)

    if final_norm:
        @pl.when(k == pl.num_programs(1) - 1)
        def _():
            fg = fg_ref[...]

            def body(c, carry):
                r = pl.multiple_of(c * NORM_ROW_CHUNK, NORM_ROW_CHUNK)
                o_ref[pl.ds(r, NORM_ROW_CHUNK), :] = _rms(o_ref[pl.ds(r, NORM_ROW_CHUNK), :], fg)
                return carry

            lax.fori_loop(0, o_ref.shape[0] // NORM_ROW_CHUNK, body, 0)


def ffn_residual(x, g, wg, wu, wd, final_g, final_norm):
    T, D = x.shape
    H = wg.shape[1]
    tm, th = FFN_TM, FFN_TH
    kernel = functools.partial(_ffn_kernel, final_norm=final_norm)
    return pl.pallas_call(
        kernel,
        grid=(T // tm, H // th),
        in_specs=[
            pl.BlockSpec((tm, D), lambda i, k: (i, 0)),
            pl.BlockSpec((1, D), lambda i, k: (0, 0)),
            pl.BlockSpec((D, th), lambda i, k: (0, k)),
            pl.BlockSpec((D, th), lambda i, k: (0, k)),
            pl.BlockSpec((th, D), lambda i, k: (k, 0)),
            pl.BlockSpec((1, D), lambda i, k: (0, 0)),
        ],
        out_specs=pl.BlockSpec((tm, D), lambda i, k: (i, 0)),
        out_shape=jax.ShapeDtypeStruct((T, D), F32),
        scratch_shapes=[pltpu.VMEM((tm, D), BF16)],
        compiler_params=_params("parallel", "arbitrary"),
        name="ffn_residual",
    )(x, g.reshape(1, D), wg, wu, wd, final_g.reshape(1, D))


def _rope_tables(S):
    t = jnp.arange(S)
    row = (t // GRID_W).astype(F32)
    col = (t % GRID_W).astype(F32)
    half = HEAD_DIM // 2
    f_row = ROPE_THETA ** (-jnp.arange(0, half, 2, dtype=F32) / half)
    f_col = ROPE_THETA ** (-jnp.arange(0, HEAD_DIM - half, 2, dtype=F32) / (HEAD_DIM - half))
    ang = jnp.concatenate([row[:, None] * f_row[None, :], col[:, None] * f_col[None, :]], axis=-1)
    cos = jnp.repeat(jnp.cos(ang), 2, axis=-1)
    sin = jnp.repeat(jnp.sin(ang), 2, axis=-1)
    sign = jnp.where(jnp.arange(HEAD_DIM) % 2 == 0, -1.0, 1.0).astype(F32)
    return cos, sin * sign[None, :]


def _lambda_init(layer_idx):
    return 0.8 - 0.6 * math.exp(-0.3 * layer_idx)


def _trunk(x, w):
    B, S, D = x.shape
    T = B * S
    h = x.reshape(T, D)
    cos, sin_signed = _rope_tables(S)
    for i in range(DEPTH):
        j = i // 2
        if i % 2 == 0:
            proj = norm_proj(h, w["attn_norm_g"][i], w["ev_w_in"][j])
            proj3 = proj.reshape(B, S, EV_IN)
            oa = diff_attention(proj3, w["ev_lambda"][j], w["ev_subln_g"][j], B, S, _lambda_init(i))
            ob = neighbourhood_attention(proj3, w["ev_rpb"][j], B, S)
            h = out_proj_residual(oa.reshape(T, A_W), ob.reshape(T, B_W),
                                  w["ev_w_out"][j][0], w["ev_w_out"][j][1], h)
        else:
            proj = od_norm_proj(h, w["attn_norm_g"][i], w["od_w_in"][j], cos, sin_signed,
                                w["od_qk_norm_g"][j].astype(F32), S)
            proj3 = proj.reshape(B, S, OD_IN)
            oc = dilated_mixture_attention(proj3, B, S).reshape(T, OC_W)
            od = gqa_attention(proj3, B, S)
            h = out_proj_residual(oc, od.reshape(T, DQ_W), w["od_w_out"][j][0], w["od_w_out"][j][1], h)
        h = ffn_residual(h, w["ffn_norm_g"][i], w["ffn_w_gate"][i], w["ffn_w_up"][i], w["ffn_w_down"][i],
                         w["final_norm_g"], final_norm=(i == DEPTH - 1))
    return h.reshape(B, S, D)


def kernel(x_prompt, x_sample, attn_norm_g, ev_w_in, ev_lambda, ev_subln_g, ev_rpb, ev_w_out, od_w_in,
           od_qk_norm_g, od_w_out, ffn_norm_g, ffn_w_gate, ffn_w_up, ffn_w_down, final_norm_g):
    w = {
        "attn_norm_g": attn_norm_g, "ffn_norm_g": ffn_norm_g, "final_norm_g": final_norm_g,
        "ev_lambda": ev_lambda, "ev_subln_g": ev_subln_g, "ev_rpb": ev_rpb, "od_qk_norm_g": od_qk_norm_g,
        "ev_w_in": [ev_w_in[j].astype(BF16) for j in range(ev_w_in.shape[0])],
        "ev_w_out": [(ev_w_out[j, :A_W].astype(BF16), ev_w_out[j, A_W:].astype(BF16))
                     for j in range(ev_w_out.shape[0])],
        "od_w_in": [od_w_in[j].astype(BF16) for j in range(od_w_in.shape[0])],
        "od_w_out": [(od_w_out[j, :OC_W].astype(BF16), od_w_out[j, OC_W:].astype(BF16))
                     for j in range(od_w_out.shape[0])],
        "ffn_w_gate": [ffn_w_gate[i].astype(BF16) for i in range(DEPTH)],
        "ffn_w_up": [ffn_w_up[i].astype(BF16) for i in range(DEPTH)],
        "ffn_w_down": [ffn_w_down[i].astype(BF16) for i in range(DEPTH)],
    }
    return _trunk(x_prompt, w), _trunk(x_sample, w)
```

```python
import functools
import math

import jax
import jax.numpy as jnp
from jax import lax
from jax.experimental import pallas as pl
from jax.experimental.pallas import tpu as pltpu

F32 = jnp.float32
BF16 = jnp.bfloat16

D_MODEL = 2048
DEPTH = 2
HEAD_DIM = 128
GRID_W = 64
EPS = 1e-6
NEG_INF = -1e30
SCALE = HEAD_DIM ** -0.5
LOG2E = math.log2(math.e)
Q_PRESCALE = SCALE * LOG2E

A_HEADS = 4
B_HEADS = 8
NA_KH = 8
NA_KW = 16
C_PATTERNS = ((128, 1), (512, 4), (2048, 16))
C_GROUP_HEADS = 4
C_HEADS = 12
D_HEADS = 12
D_KV_HEADS = 4
D_GROUP = D_HEADS // D_KV_HEADS
ROPE_THETA = 10000.0
FFN_HIDDEN = 5632

A_W = A_HEADS * 2 * HEAD_DIM
B_W = B_HEADS * HEAD_DIM
EV_IN = 3 * A_W + 3 * B_W
C_W = C_HEADS * HEAD_DIM
DQ_W = D_HEADS * HEAD_DIM
DKV_W = D_KV_HEADS * HEAD_DIM
OD_IN = 3 * C_W + DQ_W + 2 * DKV_W
OC_W = C_GROUP_HEADS * HEAD_DIM

VMEM_LIMIT_BYTES = 56 * 1024 * 1024
LANES = 128

PROJ_TM = 1024
PROJ_TN = 1024
FFN_TM = 1024
FFN_TH = 512
NORM_ROW_CHUNK = 256
ATTN_TQ = 256
DENSE_Q_TILES = 4
GQA_TK = 512
DIL_RADIUS = 64
OUT_PROJ_TM = 512


def _params(*sem):
    return pltpu.CompilerParams(dimension_semantics=sem, vmem_limit_bytes=VMEM_LIMIT_BYTES)


def _nt_dot(a, b):
    return lax.dot_general(a, b, (((1,), (1,)), ((), ())), preferred_element_type=F32)


def _rms(x, g):
    ms = jnp.mean(x * x, axis=-1, keepdims=True)
    return (x * lax.rsqrt(ms + EPS)) * g


def _norm_rows_to(x_ref, g_ref, xn_ref, rows):
    g = g_ref[...]

    def body(c, carry):
        r = pl.multiple_of(c * NORM_ROW_CHUNK, NORM_ROW_CHUNK)
        xn_ref[pl.ds(r, NORM_ROW_CHUNK), :] = _rms(x_ref[pl.ds(r, NORM_ROW_CHUNK), :], g).astype(BF16)
        return carry

    lax.fori_loop(0, rows // NORM_ROW_CHUNK, body, 0)


def _norm_proj_kernel(x_ref, g_ref, w_ref, o_ref, xn_ref, *, q_tiles):
    j = pl.program_id(1)

    @pl.when(j == 0)
    def _():
        _norm_rows_to(x_ref, g_ref, xn_ref, x_ref.shape[0])

    is_q = functools.reduce(jnp.logical_or, [j == jj for jj in q_tiles])
    factor = jnp.where(is_q, Q_PRESCALE, 1.0).astype(F32)
    acc = jnp.dot(xn_ref[...], w_ref[...], preferred_element_type=F32)
    o_ref[...] = (acc * factor).astype(o_ref.dtype)


def norm_proj(x, g, w):
    T, D = x.shape
    N = w.shape[1]
    tm, tn = PROJ_TM, PROJ_TN
    assert A_W % tn == 0 and B_W % tn == 0
    q_tiles = tuple(range(A_W // tn)) + tuple(range(3 * A_W // tn, (3 * A_W + B_W) // tn))
    return pl.pallas_call(
        functools.partial(_norm_proj_kernel, q_tiles=q_tiles),
        grid=(T // tm, N // tn),
        in_specs=[
            pl.BlockSpec((tm, D), lambda i, j: (i, 0)),
            pl.BlockSpec((1, D), lambda i, j: (0, 0)),
            pl.BlockSpec((D, tn), lambda i, j: (0, j)),
        ],
        out_specs=pl.BlockSpec((tm, tn), lambda i, j: (i, j)),
        out_shape=jax.ShapeDtypeStruct((T, N), BF16),
        scratch_shapes=[pltpu.VMEM((tm, D), BF16)],
        compiler_params=_params("parallel", "arbitrary"),
        name="norm_proj",
    )(x, g.reshape(1, D), w)


def _qk_norm_rope(x, g, cos, sin_signed):
    y = _rms(x, g)
    lane = lax.broadcasted_iota(jnp.int32, y.shape, 1)
    nxt = pltpu.roll(y, LANES - 1, 1)
    prv = pltpu.roll(y, 1, 1)
    partner = jnp.where((lane & 1) == 0, nxt, prv)
    return y * cos + partner * sin_signed


KIND_DILATED_Q, KIND_GQA_Q, KIND_GQA_K = "dilated_q", "gqa_q", "gqa_k"


def _od_head_kinds(tn):
    slots = tn // LANES
    c_hi = C_W // LANES
    q_lo, q_hi = 3 * C_W // LANES, (3 * C_W + DQ_W) // LANES
    k_hi = q_hi + DKV_W // LANES
    table = {}
    for j in range(OD_IN // tn):
        kinds = []
        for c in range(slots):
            s = j * slots + c
            kinds.append(KIND_DILATED_Q if s < c_hi else KIND_GQA_Q if q_lo <= s < q_hi
                         else KIND_GQA_K if q_hi <= s < k_hi else None)
        if any(k is not None for k in kinds):
            table[j] = kinds
    return table


def _od_proj_kernel(x_ref, g_ref, w_ref, cos_ref, sin_ref, qkg_ref, o_ref, xn_ref, *, kinds_by_tile):
    j = pl.program_id(1)

    @pl.when(j == 0)
    def _():
        _norm_rows_to(x_ref, g_ref, xn_ref, x_ref.shape[0])

    acc = jnp.dot(xn_ref[...], w_ref[...], preferred_element_type=F32)

    plain = None
    for jj, kinds in kinds_by_tile.items():
        hit = j == jj
        plain = jnp.logical_not(hit) if plain is None else jnp.logical_and(plain, jnp.logical_not(hit))

        @pl.when(hit)
        def _(kinds=kinds):
            cos = cos_ref[...]
            sin = sin_ref[...]
            for c, kind in enumerate(kinds):
                blk = acc[:, c * LANES:(c + 1) * LANES]
                if kind == KIND_GQA_Q:
                    blk = _qk_norm_rope(blk, qkg_ref[0:1, :], cos, sin) * Q_PRESCALE
                elif kind == KIND_GQA_K:
                    blk = _qk_norm_rope(blk, qkg_ref[1:2, :], cos, sin)
                elif kind == KIND_DILATED_Q:
                    blk = blk * Q_PRESCALE
                o_ref[:, c * LANES:(c + 1) * LANES] = blk.astype(o_ref.dtype)

    @pl.when(plain)
    def _():
        o_ref[...] = acc.astype(o_ref.dtype)


def od_norm_proj(x, g, w, cos, sin_signed, qk_g, S):
    T, D = x.shape
    N = w.shape[1]
    tm, tn = PROJ_TM, PROJ_TN
    s_tiles = S // tm
    kernel = functools.partial(_od_proj_kernel, kinds_by_tile=_od_head_kinds(tn))
    return pl.pallas_call(
        kernel,
        grid=(T // tm, N // tn),
        in_specs=[
            pl.BlockSpec((tm, D), lambda i, j: (i, 0)),
            pl.BlockSpec((1, D), lambda i, j: (0, 0)),
            pl.BlockSpec((D, tn), lambda i, j: (0, j)),
            pl.BlockSpec((tm, HEAD_DIM), lambda i, j: (i % s_tiles, 0)),
            pl.BlockSpec((tm, HEAD_DIM), lambda i, j: (i % s_tiles, 0)),
            pl.BlockSpec((2, HEAD_DIM), lambda i, j: (0, 0)),
        ],
        out_specs=pl.BlockSpec((tm, tn), lambda i, j: (i, j)),
        out_shape=jax.ShapeDtypeStruct((T, N), BF16),
        scratch_shapes=[pltpu.VMEM((tm, D), BF16)],
        compiler_params=_params("parallel", "arbitrary"),
        name="od_norm_proj",
    )(x, g.reshape(1, D), w, cos, sin_signed, qk_g)


def _online_softmax(n, q_chunk, k_chunk, v_chunk, bias_chunk, n_chunks, row_sums):
    m = [None] * n
    acc = [None] * n
    den = [None] * n
    for j in range(n_chunks):
        for c in range(n):
            s = _nt_dot(q_chunk(c, j), k_chunk(c, j))
            bias = bias_chunk(c, j)
            if bias is not None:
                s = s + bias
            m_j = jnp.max(s, axis=-1, keepdims=True)
            m_new = m_j if j == 0 else jnp.maximum(m[c], m_j)
            p = jnp.exp2(s - m_new).astype(BF16)
            p_sum = jnp.sum(p.astype(F32), axis=-1, keepdims=True) if row_sums else None
            pv = jnp.dot(p, v_chunk(c, j), preferred_element_type=F32)
            if j == 0:
                acc[c] = pv
                den[c] = p_sum
            else:
                alpha = jnp.exp2(m[c] - m_new)
                acc[c] = alpha * acc[c] + pv
                if row_sums:
                    den[c] = alpha * den[c] + p_sum
            m[c] = m_new
    return acc, den


def _fill_value_and_ones(v_ref, va_ref):
    rows, d = v_ref.shape
    va_ref[:, :d] = v_ref[...]
    va_ref[:, d:] = jnp.ones((rows, va_ref.shape[1] - d), va_ref.dtype)


ALIBI_SPLIT = 3
ALIBI_POS_BITS = 6


def _bf16_terms(x):
    terms, rest = [], x
    for _ in range(ALIBI_SPLIT):
        t = rest.astype(BF16).astype(F32)
        terms.append(t)
        rest = rest - t
    return terms


def _alibi_features(pos, slope_terms, lane, key_side):
    hi = (pos >> ALIBI_POS_BITS).astype(F32)
    lo = (pos & ((1 << ALIBI_POS_BITS) - 1)).astype(F32)
    big = float(1 << ALIBI_POS_BITS)
    out = jnp.zeros(lane.shape, F32)
    for t, c in enumerate(slope_terms):
        if key_side:
            vals = (-big * c, -c, hi, lo)
        else:
            vals = (hi, lo, big * c, c)
        for group, v in enumerate(vals):
            out = jnp.where(lane == group * ALIBI_SPLIT + t, v, out)
    return out


def _diff_attn_kernel(slopes_ref, lam_ref, g_ref, q_ref, k_ref, v_ref, o_ref, bias_ref, kx_ref,
                      *, S, tq, n_sub, tk, lambda_init):
    h = pl.program_id(0)
    b = pl.program_id(1)
    i = pl.program_id(2)
    width = 2 * tk - tq
    chunk = min(tq, 2 * LANES)
    n_chunks = S // tk
    d = HEAD_DIM
    slope_terms = _bf16_terms(jnp.full((1, LANES), slopes_ref[h] * LOG2E, F32))

    @pl.when(i == 0)
    def _():
        for c in range(2):
            kx_ref[c, :, :d] = k_ref[:, c * d:(c + 1) * d]

    @pl.when(jnp.logical_and(b == 0, i == 0))
    def _():
        slope = slopes_ref[h] * LOG2E
        row = lax.broadcasted_iota(jnp.int32, (tq, chunk), 0)
        col = lax.broadcasted_iota(jnp.int32, (tq, chunk), 1)
        base = row + (tk - tq) - col

        def body(c, carry):
            off = pl.multiple_of(c * chunk, chunk)
            dist = jnp.abs(base - off).astype(F32)
            bias_ref[:, pl.ds(off, chunk)] = -(slope * dist)
            return carry

        lax.fori_loop(0, width // chunk, body, 0)

        def key_body(c, carry):
            off = pl.multiple_of(c * chunk, chunk)
            pos = off + lax.broadcasted_iota(jnp.int32, (chunk, LANES), 0)
            lane = lax.broadcasted_iota(jnp.int32, (chunk, LANES), 1)
            feat = _alibi_features(pos, slope_terms, lane, key_side=True).astype(BF16)
            kx_ref[0, pl.ds(off, chunk), d:] = feat
            kx_ref[1, pl.ds(off, chunk), d:] = feat
            return carry

        lax.fori_loop(0, S // chunk, key_body, 0)

    lv = lam_ref[...]
    lam = (jnp.exp(jnp.sum(lv[0:1] * lv[1:2], axis=-1, keepdims=True))
           - jnp.exp(jnp.sum(lv[2:3] * lv[3:4], axis=-1, keepdims=True)) + lambda_init)

    assert n_sub * tq == tk
    starts = [tk - tq - r * tq for r in range(n_sub)]
    lane = lax.broadcasted_iota(jnp.int32, (tq, LANES), 1)
    feats = []
    for r in range(n_sub):
        pos = (i * n_sub + r) * tq + lax.broadcasted_iota(jnp.int32, (tq, LANES), 0)
        feats.append(_alibi_features(pos, slope_terms, lane, key_side=False))
    zeros = jnp.zeros((tq, LANES), BF16)

    def key_chunk_index(role):
        cj = i + role
        return jnp.where(cj >= n_chunks, cj - n_chunks, cj)

    def q_chunk(n, role):
        r, c = n // 2, n % 2
        q = q_ref[r * tq:(r + 1) * tq, c * d:(c + 1) * d]
        if role == 0:
            return jnp.concatenate([q, zeros], axis=1)
        sign = jnp.where(i + role >= n_chunks, 1.0, -1.0)
        return jnp.concatenate([q, (feats[r] * sign).astype(BF16)], axis=1)

    def k_chunk(n, role):
        return kx_ref[n % 2, pl.ds(pl.multiple_of(key_chunk_index(role) * tk, tk), tk), :]

    def v_chunk(n, role):
        return v_ref[pl.ds(pl.multiple_of(key_chunk_index(role) * tk, tk), tk), :]

    def bias_chunk(n, role):
        return bias_ref[:, starts[n // 2]:starts[n // 2] + tk] if role == 0 else None

    acc, den = _online_softmax(2 * n_sub, q_chunk, k_chunk, v_chunk, bias_chunk, n_chunks, row_sums=True)
    for r in range(n_sub):
        o = acc[2 * r] / den[2 * r] - lam * (acc[2 * r + 1] / den[2 * r + 1])
        o = _rms(o, g_ref[...]) * (1.0 - lambda_init)
        o_ref[r * tq:(r + 1) * tq, :] = o.astype(o_ref.dtype)


def diff_attention(proj, lam_vec, subln_g, B, S, lambda_init):
    tq = ATTN_TQ
    dv = 2 * HEAD_DIM
    slopes = jnp.asarray([2.0 ** (-8.0 * (i + 1) / A_HEADS) for i in range(A_HEADS)], F32)
    n_sub = DENSE_Q_TILES
    tk = tq * n_sub
    kernel = functools.partial(_diff_attn_kernel, S=S, tq=tq, n_sub=n_sub, tk=tk, lambda_init=lambda_init)
    return pl.pallas_call(
        kernel,
        grid=(A_HEADS, B, S // (tq * n_sub)),
        in_specs=[
            pl.BlockSpec(memory_space=pltpu.SMEM),
            pl.BlockSpec((4, HEAD_DIM), lambda h, b, i: (0, 0)),
            pl.BlockSpec((1, dv), lambda h, b, i: (0, 0)),
            pl.BlockSpec((None, tq * n_sub, dv), lambda h, b, i: (b, i, h)),
            pl.BlockSpec((None, S, dv), lambda h, b, i: (b, 0, A_HEADS + h)),
            pl.BlockSpec((None, S, dv), lambda h, b, i: (b, 0, 2 * A_HEADS + h)),
        ],
        out_specs=pl.BlockSpec((None, tq * n_sub, dv), lambda h, b, i: (b, i, h)),
        out_shape=jax.ShapeDtypeStruct((B, S, A_W), BF16),
        scratch_shapes=[pltpu.VMEM((tq, 2 * tk - tq), F32), pltpu.VMEM((2, S, 2 * HEAD_DIM), BF16)],
        compiler_params=_params("arbitrary", "arbitrary", "arbitrary"),
        name="diff_attention",
    )(slopes, lam_vec.astype(F32), subln_g.reshape(1, dv), proj, proj, proj)


RPB_ROWS = 2 * NA_KH - 1
RPB_COLS = 2 * NA_KW - 1
NA_Q_ROWS = 4
NA_K_ROWS = 12
NA_Q = NA_Q_ROWS * GRID_W
NA_K = NA_K_ROWS * GRID_W
NA_GROUPS_PER_STEP = 4


def _na_case_rule(case, a):
    if case == 0:
        return 0, NA_KH, NA_KH - 1 - a
    if case == 1:
        return a, a + NA_KH, NA_KH // 2 - 1 - a
    return NA_K_ROWS - NA_KH, NA_K_ROWS, -1 - a


def _nbr_attn_kernel(rpb_ref, q_ref, k_ref, v_ref, o_ref, pair_ref, tab_ref, *, S):
    h = pl.program_id(0)
    b = pl.program_id(1)
    rows = S // GRID_W
    n_groups = rows // NA_Q_ROWS

    @pl.when(b == 0)
    def _():
        c = lax.broadcasted_iota(jnp.int32, (GRID_W, LANES), 0)
        lane = lax.broadcasted_iota(jnp.int32, (GRID_W, LANES), 1)
        w = lane & (GRID_W - 1)
        upper = lane >= GRID_W
        c0 = jnp.clip(c - NA_KW // 2, 0, GRID_W - NA_KW)
        valid = jnp.logical_and(w >= c0, w < c0 + NA_KW)
        dc = jnp.clip(w - c, 1 - NA_KW, NA_KW - 1) + NA_KW - 1
        base = h * (RPB_ROWS * RPB_COLS)
        for e in range(RPB_ROWS + 1):
            def body(d, acc, e=e):
                lo = rpb_ref[base + (e - 1) * RPB_COLS + d] * LOG2E if e >= 1 else 0.0
                hi = rpb_ref[base + e * RPB_COLS + d] * LOG2E if e < RPB_ROWS else 0.0
                return jnp.where(dc == d, jnp.where(upper, hi, lo), acc)

            acc = lax.fori_loop(0, RPB_COLS, body, jnp.zeros((GRID_W, LANES), F32))
            pair_ref[e] = jnp.where(valid, acc, NEG_INF)

        for case in range(3):
            for a in range(NA_Q_ROWS):
                lo_slot, hi_slot, off = _na_case_rule(case, a)
                for t in range(NA_K_ROWS // 2):
                    see0 = lo_slot <= 2 * t < hi_slot
                    see1 = lo_slot <= 2 * t + 1 < hi_slot
                    if see0 or see1:
                        tile = pair_ref[2 * t + off + 1]
                        if not see0:
                            tile = jnp.where(upper, tile, NEG_INF)
                        if not see1:
                            tile = jnp.where(upper, NEG_INF, tile)
                    else:
                        tile = jnp.full((GRID_W, LANES), NEG_INF, F32)
                    tab_ref[case, a * GRID_W:(a + 1) * GRID_W, t * LANES:(t + 1) * LANES] = tile

    def group(gi):
        case = jnp.where(gi == 0, 0, jnp.where(gi == n_groups - 1, 2, 1))
        u0 = jnp.clip(gi * NA_Q_ROWS - NA_KH // 2, 0, rows - NA_K_ROWS)
        qs = pl.multiple_of(gi * NA_Q, NA_Q)
        ks = pl.multiple_of(u0 * GRID_W, GRID_W)
        q = q_ref[pl.ds(qs, NA_Q), :]
        k = k_ref[pl.ds(ks, NA_K), :]
        v = v_ref[pl.ds(ks, NA_K), :]
        s = _nt_dot(q, k) + tab_ref[case]
        m = jnp.max(s, axis=-1, keepdims=True)
        p = jnp.exp2(s - m)
        l = jnp.sum(p, axis=-1, keepdims=True)
        o = jnp.dot(p.astype(BF16), v, preferred_element_type=F32) / l
        o_ref[pl.ds(qs, NA_Q), :] = o.astype(o_ref.dtype)

    def step(gp, carry):
        for a in range(NA_GROUPS_PER_STEP):
            group(NA_GROUPS_PER_STEP * gp + a)
        return carry

    lax.fori_loop(0, n_groups // NA_GROUPS_PER_STEP, step, 0)


def neighbourhood_attention(proj, rpb, B, S):
    q0 = 3 * A_W // HEAD_DIM
    kernel = functools.partial(_nbr_attn_kernel, S=S)
    return pl.pallas_call(
        kernel,
        grid=(B_HEADS, B),
        in_specs=[
            pl.BlockSpec(memory_space=pltpu.SMEM),
            pl.BlockSpec((None, S, HEAD_DIM), lambda h, b: (b, 0, q0 + h)),
            pl.BlockSpec((None, S, HEAD_DIM), lambda h, b: (b, 0, q0 + B_HEADS + h)),
            pl.BlockSpec((None, S, HEAD_DIM), lambda h, b: (b, 0, q0 + 2 * B_HEADS + h)),
        ],
        out_specs=pl.BlockSpec((None, S, HEAD_DIM), lambda h, b: (b, 0, h)),
        out_shape=jax.ShapeDtypeStruct((B, S, B_W), BF16),
        scratch_shapes=[pltpu.VMEM((RPB_ROWS + 1, GRID_W, LANES), F32),
                        pltpu.VMEM((3, NA_Q, NA_K), F32)],
        compiler_params=_params("arbitrary", "arbitrary"),
        name="neighbourhood_attention",
    )(rpb.astype(F32).reshape(-1), proj, proj, proj)


def _dil_geometry(S, tl, dil):
    half = DIL_RADIUS * dil
    hp = -(-half // LANES) * LANES
    kw = min(S, tl + 2 * hp)
    deltas = [min(max(l0 - hp, 0), S - kw) - l0 for l0 in range(0, S, tl)]
    u0 = -min(deltas)
    return half, hp, kw, u0, kw + max(deltas) + u0


def _dilated_kernel(slopes_ref, q0_ref, q1_ref, q2_ref, k0_ref, k1_ref, k2_ref, v0_ref, v1_ref, v2_ref,
                    o_ref, b0_ref, b1_ref, b2_ref, *, S, tl, n_sub, geoms):
    hh = pl.program_id(0)
    b = pl.program_id(1)
    i = pl.program_id(2)
    q_refs, k_refs, v_refs = (q0_ref, q1_ref, q2_ref), (k0_ref, k1_ref, k2_ref), (v0_ref, v1_ref, v2_ref)
    bias_refs = (b0_ref, b1_ref, b2_ref)

    @pl.when(jnp.logical_and(b == 0, i == 0))
    def _():
        for g, (_, dil) in enumerate(C_PATTERNS):
            half, _, _, u0, width = geoms[g]
            slope = slopes_ref[g * C_GROUP_HEADS + hh] * LOG2E
            row = lax.broadcasted_iota(jnp.int32, (tl, LANES), 0)
            col = lax.broadcasted_iota(jnp.int32, (tl, LANES), 1)
            base = col - row - u0

            def body(c, carry, g=g, dil=dil, half=half, slope=slope, base=base):
                off = pl.multiple_of(c * LANES, LANES)
                rel = base + off
                dist = jnp.abs(rel)
                ok = jnp.logical_and(dist <= half, (rel & (dil - 1)) == 0)
                bias_refs[g][:, pl.ds(off, LANES)] = jnp.where(ok, -(slope * dist.astype(F32)), NEG_INF)
                return carry

            lax.fori_loop(0, width // LANES, body, 0)

    for r in range(n_sub):
        rows = slice(r * tl, (r + 1) * tl)
        l0 = (i * n_sub + r) * tl
        accs, ms, ls = [], [], []
        for g in range(len(C_PATTERNS)):
            _, hp, kw, u0, _ = geoms[g]
            ws = pl.multiple_of(jnp.clip(l0 - hp, 0, S - kw), LANES)
            start = pl.multiple_of(ws - l0 + u0, LANES)
            k = k_refs[g][pl.ds(ws, kw), :]
            v = v_refs[g][pl.ds(ws, kw), :]
            s = _nt_dot(q_refs[g][rows, :], k) + bias_refs[g][:, pl.ds(start, kw)]
            m = jnp.max(s, axis=-1, keepdims=True)
            p = jnp.exp2(s - m)
            ls.append(jnp.sum(p, axis=-1, keepdims=True))
            ms.append(m)
            accs.append(jnp.dot(p.astype(BF16), v, preferred_element_type=F32))

        m_all = jnp.maximum(jnp.maximum(ms[0], ms[1]), ms[2])
        ws_ = [jnp.exp2(m - m_all) for m in ms]
        num = ws_[0] * accs[0] + ws_[1] * accs[1] + ws_[2] * accs[2]
        den = ws_[0] * ls[0] + ws_[1] * ls[1] + ws_[2] * ls[2]
        o_ref[rows, :] = (num / den).astype(o_ref.dtype)


def dilated_mixture_attention(proj, B, S):
    tl = ATTN_TQ
    ng = len(C_PATTERNS)
    for win, dil in C_PATTERNS:
        assert win // (2 * dil) == DIL_RADIUS and dil & (dil - 1) == 0
    geoms = tuple(_dil_geometry(S, tl, dil) for _, dil in C_PATTERNS)
    slopes = jnp.asarray([2.0 ** (-8.0 * (i + 1) / C_HEADS) for i in range(C_HEADS)], F32)
    k0, v0 = C_W // HEAD_DIM, 2 * C_W // HEAD_DIM

    n_sub = DENSE_Q_TILES

    def q_spec(g):
        return pl.BlockSpec((None, tl * n_sub, HEAD_DIM), lambda hh, b, i: (b, i, g * C_GROUP_HEADS + hh))

    def kv_spec(first, g):
        return pl.BlockSpec((None, S, HEAD_DIM), lambda hh, b, i: (b, 0, first + g * C_GROUP_HEADS + hh))

    kernel = functools.partial(_dilated_kernel, S=S, tl=tl, n_sub=n_sub, geoms=geoms)
    return pl.pallas_call(
        kernel,
        grid=(C_GROUP_HEADS, B, S // (tl * n_sub)),
        in_specs=([pl.BlockSpec(memory_space=pltpu.SMEM)]
                  + [q_spec(g) for g in range(ng)]
                  + [kv_spec(k0, g) for g in range(ng)]
                  + [kv_spec(v0, g) for g in range(ng)]),
        out_specs=pl.BlockSpec((None, tl * n_sub, HEAD_DIM), lambda hh, b, i: (b, i, hh)),
        out_shape=jax.ShapeDtypeStruct((B, S, OC_W), BF16),
        scratch_shapes=[pltpu.VMEM((tl, geoms[g][4]), F32) for g in range(ng)],
        compiler_params=_params("arbitrary", "arbitrary", "arbitrary"),
        name="dilated_mixture_attention",
    )(slopes, *([proj] * (3 * ng)))


def _gqa_kernel(q_ref, k_ref, v_ref, o_ref, va_ref, *, S, tq, n_sub, tk):
    d = HEAD_DIM

    @pl.when(pl.program_id(2) == 0)
    def _():
        _fill_value_and_ones(v_ref, va_ref)

    chains = [(r, g) for r in range(n_sub) for g in range(D_GROUP)]
    acc, _ = _online_softmax(
        len(chains),
        lambda n, j: q_ref[chains[n][0] * tq:(chains[n][0] + 1) * tq, chains[n][1] * d:(chains[n][1] + 1) * d],
        lambda n, j: k_ref[j * tk:(j + 1) * tk, :],
        lambda n, j: va_ref[j * tk:(j + 1) * tk, :],
        lambda n, j: None,
        S // tk, row_sums=False)
    for n, (r, g) in enumerate(chains):
        o_ref[r * tq:(r + 1) * tq, g * d:(g + 1) * d] = (acc[n][:, :d] / acc[n][:, d:]).astype(o_ref.dtype)


def gqa_attention(proj, B, S):
    tq = ATTN_TQ
    qw = D_GROUP * HEAD_DIM
    q0 = 3 * C_W // qw
    k0 = (3 * C_W + DQ_W) // HEAD_DIM
    v0 = k0 + D_KV_HEADS
    n_sub = DENSE_Q_TILES
    tk = min(GQA_TK, S)
    return pl.pallas_call(
        functools.partial(_gqa_kernel, S=S, tq=tq, n_sub=n_sub, tk=tk),
        grid=(D_KV_HEADS, B, S // (tq * n_sub)),
        in_specs=[
            pl.BlockSpec((None, tq * n_sub, qw), lambda h, b, i: (b, i, q0 + h)),
            pl.BlockSpec((None, S, HEAD_DIM), lambda h, b, i: (b, 0, k0 + h)),
            pl.BlockSpec((None, S, HEAD_DIM), lambda h, b, i: (b, 0, v0 + h)),
        ],
        out_specs=pl.BlockSpec((None, tq * n_sub, qw), lambda h, b, i: (b, i, h)),
        out_shape=jax.ShapeDtypeStruct((B, S, DQ_W), BF16),
        scratch_shapes=[pltpu.VMEM((S, 2 * HEAD_DIM), BF16)],
        compiler_params=_params("arbitrary", "arbitrary", "arbitrary"),
        name="gqa_attention",
    )(proj, proj, proj)


def _out_proj_kernel(a1_ref, a2_ref, w1_ref, w2_ref, h_ref, o_ref):
    acc = jnp.dot(a1_ref[...], w1_ref[...], preferred_element_type=F32)
    acc = acc + jnp.dot(a2_ref[...], w2_ref[...], preferred_element_type=F32)
    o_ref[...] = h_ref[...] + acc


def out_proj_residual(a1, a2, w1, w2, h):
    T, D = h.shape
    k1, k2 = a1.shape[1], a2.shape[1]
    tm = OUT_PROJ_TM
    resident = pl.Buffered(1)
    return pl.pallas_call(
        _out_proj_kernel,
        grid=(T // tm,),
        in_specs=[
            pl.BlockSpec((tm, k1), lambda i: (i, 0)),
            pl.BlockSpec((tm, k2), lambda i: (i, 0)),
            pl.BlockSpec((k1, D), lambda i: (0, 0), pipeline_mode=resident),
            pl.BlockSpec((k2, D), lambda i: (0, 0), pipeline_mode=resident),
            pl.BlockSpec((tm, D), lambda i: (i, 0)),
        ],
        out_specs=pl.BlockSpec((tm, D), lambda i: (i, 0)),
        out_shape=jax.ShapeDtypeStruct((T, D), F32),
        compiler_params=_params("parallel"),
        name="out_proj_residual",
    )(a1, a2, w1, w2, h)


def _ffn_kernel(x_ref, g_ref, wg_ref, wu_ref, wd_ref, fg_ref, o_ref, xn_ref, *, final_norm):
    k = pl.program_id(1)

    @pl.when(k == 0)
    def _():
        _norm_rows_to(x_ref, g_ref, xn_ref, x_ref.shape[0])
        o_ref[...] = x_ref[...]

    xn = xn_ref[...]
    gate = jnp.dot(xn, wg_ref[...], preferred_element_type=F32)
    up = jnp.dot(xn, wu_ref[...], preferred_element_type=F32)
    act = (gate * jax.nn.sigmoid(gate)) * up
    o_ref[...] += jnp.dot(act.astype(BF16), wd_ref[...], preferred_element_type=F32)

    if final_norm:
        @pl.when(k == pl.num_programs(1) - 1)
        def _():
            fg = fg_ref[...]

            def body(c, carry):
                r = pl.multiple_of(c * NORM_ROW_CHUNK, NORM_ROW_CHUNK)
                o_ref[pl.ds(r, NORM_ROW_CHUNK), :] = _rms(o_ref[pl.ds(r, NORM_ROW_CHUNK), :], fg)
                return carry

            lax.fori_loop(0, o_ref.shape[0] // NORM_ROW_CHUNK, body, 0)


def ffn_residual(x, g, wg, wu, wd, final_g, final_norm):
    T, D = x.shape
    H = wg.shape[1]
    tm, th = FFN_TM, FFN_TH
    kernel = functools.partial(_ffn_kernel, final_norm=final_norm)
    return pl.pallas_call(
        kernel,
        grid=(T // tm, H // th),
        in_specs=[
            pl.BlockSpec((tm, D), lambda i, k: (i, 0)),
            pl.BlockSpec((1, D), lambda i, k: (0, 0)),
            pl.BlockSpec((D, th), lambda i, k: (0, k)),
            pl.BlockSpec((D, th), lambda i, k: (0, k)),
            pl.BlockSpec((th, D), lambda i, k: (k, 0)),
            pl.BlockSpec((1, D), lambda i, k: (0, 0)),
        ],
        out_specs=pl.BlockSpec((tm, D), lambda i, k: (i, 0)),
        out_shape=jax.ShapeDtypeStruct((T, D), F32),
        scratch_shapes=[pltpu.VMEM((tm, D), BF16)],
        compiler_params=_params("parallel", "arbitrary"),
        name="ffn_residual",
    )(x, g.reshape(1, D), wg, wu, wd, final_g.reshape(1, D))


def _rope_tables(S):
    t = jnp.arange(S)
    row = (t // GRID_W).astype(F32)
    col = (t % GRID_W).astype(F32)
    half = HEAD_DIM // 2
    f_row = ROPE_THETA ** (-jnp.arange(0, half, 2, dtype=F32) / half)
    f_col = ROPE_THETA ** (-jnp.arange(0, HEAD_DIM - half, 2, dtype=F32) / (HEAD_DIM - half))
    ang = jnp.concatenate([row[:, None] * f_row[None, :], col[:, None] * f_col[None, :]], axis=-1)
    cos = jnp.repeat(jnp.cos(ang), 2, axis=-1)
    sin = jnp.repeat(jnp.sin(ang), 2, axis=-1)
    sign = jnp.where(jnp.arange(HEAD_DIM) % 2 == 0, -1.0, 1.0).astype(F32)
    return cos, sin * sign[None, :]


def _lambda_init(layer_idx):
    return 0.8 - 0.6 * math.exp(-0.3 * layer_idx)


def _trunk(x, w):
    B, S, D = x.shape
    T = B * S
    h = x.reshape(T, D)
    cos, sin_signed = _rope_tables(S)
    for i in range(DEPTH):
        j = i // 2
        if i % 2 == 0:
            proj = norm_proj(h, w["attn_norm_g"][i], w["ev_w_in"][j])
            proj3 = proj.reshape(B, S, EV_IN)
            oa = diff_attention(proj3, w["ev_lambda"][j], w["ev_subln_g"][j], B, S, _lambda_init(i))
            ob = neighbourhood_attention(proj3, w["ev_rpb"][j], B, S)
            h = out_proj_residual(oa.reshape(T, A_W), ob.reshape(T, B_W),
                                  w["ev_w_out"][j][0], w["ev_w_out"][j][1], h)
        else:
            proj = od_norm_proj(h, w["attn_norm_g"][i], w["od_w_in"][j], cos, sin_signed,
                                w["od_qk_norm_g"][j].astype(F32), S)
            proj3 = proj.reshape(B, S, OD_IN)
            oc = dilated_mixture_attention(proj3, B, S).reshape(T, OC_W)
            od = gqa_attention(proj3, B, S)
            h = out_proj_residual(oc, od.reshape(T, DQ_W), w["od_w_out"][j][0], w["od_w_out"][j][1], h)
        h = ffn_residual(h, w["ffn_norm_g"][i], w["ffn_w_gate"][i], w["ffn_w_up"][i], w["ffn_w_down"][i],
                         w["final_norm_g"], final_norm=(i == DEPTH - 1))
    return h.reshape(B, S, D)


def kernel(x_prompt, x_sample, attn_norm_g, ev_w_in, ev_lambda, ev_subln_g, ev_rpb, ev_w_out, od_w_in,
           od_qk_norm_g, od_w_out, ffn_norm_g, ffn_w_gate, ffn_w_up, ffn_w_down, final_norm_g):
    w = {
        "attn_norm_g": attn_norm_g, "ffn_norm_g": ffn_norm_g, "final_norm_g": final_norm_g,
        "ev_lambda": ev_lambda, "ev_subln_g": ev_subln_g, "ev_rpb": ev_rpb, "od_qk_norm_g": od_qk_norm_g,
        "ev_w_in": [ev_w_in[j].astype(BF16) for j in range(ev_w_in.shape[0])],
        "ev_w_out": [(ev_w_out[j, :A_W].astype(BF16), ev_w_out[j, A_W:].astype(BF16))
                     for j in range(ev_w_out.shape[0])],
        "od_w_in": [od_w_in[j].astype(BF16) for j in range(od_w_in.shape[0])],
        "od_w_out": [(od_w_out[j, :OC_W].astype(BF16), od_w_out[j, OC_W:].astype(BF16))
                     for j in range(od_w_out.shape[0])],
        "ffn_w_gate": [ffn_w_gate[i].astype(BF16) for i in range(DEPTH)],
        "ffn_w_up": [ffn_w_up[i].astype(BF16) for i in range(DEPTH)],
        "ffn_w_down": [ffn_w_down[i].astype(BF16) for i in range(DEPTH)],
    }
    return _trunk(x_prompt, w), _trunk(x_sample, w)
```

```python
import functools
import math

import jax
import jax.numpy as jnp
from jax import lax
from jax.experimental import pallas as pl
from jax.experimental.pallas import tpu as pltpu

F32 = jnp.float32
BF16 = jnp.bfloat16

D_MODEL = 2048
DEPTH = 2
HEAD_DIM = 128
GRID_W = 64
EPS = 1e-6
NEG_INF = -1e30
SCALE = HEAD_DIM ** -0.5
LOG2E = math.log2(math.e)
Q_PRESCALE = SCALE * LOG2E

A_HEADS = 4
B_HEADS = 8
NA_KH = 8
NA_KW = 16
C_PATTERNS = ((128, 1), (512, 4), (2048, 16))
C_GROUP_HEADS = 4
C_HEADS = 12
D_HEADS = 12
D_KV_HEADS = 4
D_GROUP = D_HEADS // D_KV_HEADS
ROPE_THETA = 10000.0
FFN_HIDDEN = 5632

A_W = A_HEADS * 2 * HEAD_DIM
B_W = B_HEADS * HEAD_DIM
EV_IN = 3 * A_W + 3 * B_W
C_W = C_HEADS * HEAD_DIM
DQ_W = D_HEADS * HEAD_DIM
DKV_W = D_KV_HEADS * HEAD_DIM
OD_IN = 3 * C_W + DQ_W + 2 * DKV_W
OC_W = C_GROUP_HEADS * HEAD_DIM

VMEM_LIMIT_BYTES = 56 * 1024 * 1024
LANES = 128

PROJ_TM = 1024
PROJ_TN = 1024
FFN_TM = 1024
FFN_TH = 512
NORM_ROW_CHUNK = 256
ATTN_TQ = 256
DENSE_Q_TILES = 4
GQA_TK = 512
DIL_RADIUS = 64
OUT_PROJ_TM = 512


def _params(*sem):
    return pltpu.CompilerParams(dimension_semantics=sem, vmem_limit_bytes=VMEM_LIMIT_BYTES)


def _nt_dot(a, b):
    return lax.dot_general(a, b, (((1,), (1,)), ((), ())), preferred_element_type=F32)


def _rms(x, g):
    ms = jnp.mean(x * x, axis=-1, keepdims=True)
    return (x * lax.rsqrt(ms + EPS)) * g


def _norm_rows_to(x_ref, g_ref, xn_ref, rows):
    g = g_ref[...]

    def body(c, carry):
        r = pl.multiple_of(c * NORM_ROW_CHUNK, NORM_ROW_CHUNK)
        xn_ref[pl.ds(r, NORM_ROW_CHUNK), :] = _rms(x_ref[pl.ds(r, NORM_ROW_CHUNK), :], g).astype(BF16)
        return carry

    lax.fori_loop(0, rows // NORM_ROW_CHUNK, body, 0)


def _norm_proj_kernel(x_ref, g_ref, w_ref, o_ref, xn_ref, *, q_tiles):
    j = pl.program_id(1)

    @pl.when(j == 0)
    def _():
        _norm_rows_to(x_ref, g_ref, xn_ref, x_ref.shape[0])

    is_q = functools.reduce(jnp.logical_or, [j == jj for jj in q_tiles])
    factor = jnp.where(is_q, Q_PRESCALE, 1.0).astype(F32)
    acc = jnp.dot(xn_ref[...], w_ref[...], preferred_element_type=F32)
    o_ref[...] = (acc * factor).astype(o_ref.dtype)


def norm_proj(x, g, w):
    T, D = x.shape
    N = w.shape[1]
    tm, tn = PROJ_TM, PROJ_TN
    assert A_W % tn == 0 and B_W % tn == 0
    q_tiles = tuple(range(A_W // tn)) + tuple(range(3 * A_W // tn, (3 * A_W + B_W) // tn))
    return pl.pallas_call(
        functools.partial(_norm_proj_kernel, q_tiles=q_tiles),
        grid=(T // tm, N // tn),
        in_specs=[
            pl.BlockSpec((tm, D), lambda i, j: (i, 0)),
            pl.BlockSpec((1, D), lambda i, j: (0, 0)),
            pl.BlockSpec((D, tn), lambda i, j: (0, j)),
        ],
        out_specs=pl.BlockSpec((tm, tn), lambda i, j: (i, j)),
        out_shape=jax.ShapeDtypeStruct((T, N), BF16),
        scratch_shapes=[pltpu.VMEM((tm, D), BF16)],
        compiler_params=_params("parallel", "arbitrary"),
        name="norm_proj",
    )(x, g.reshape(1, D), w)


def _qk_norm_rope(x, g, cos, sin_signed):
    y = _rms(x, g)
    lane = lax.broadcasted_iota(jnp.int32, y.shape, 1)
    nxt = pltpu.roll(y, LANES - 1, 1)
    prv = pltpu.roll(y, 1, 1)
    partner = jnp.where((lane & 1) == 0, nxt, prv)
    return y * cos + partner * sin_signed


KIND_DILATED_Q, KIND_GQA_Q, KIND_GQA_K = "dilated_q", "gqa_q", "gqa_k"


def _od_head_kinds(tn):
    slots = tn // LANES
    c_hi = C_W // LANES
    q_lo, q_hi = 3 * C_W // LANES, (3 * C_W + DQ_W) // LANES
    k_hi = q_hi + DKV_W // LANES
    table = {}
    for j in range(OD_IN // tn):
        kinds = []
        for c in range(slots):
            s = j * slots + c
            kinds.append(KIND_DILATED_Q if s < c_hi else KIND_GQA_Q if q_lo <= s < q_hi
                         else KIND_GQA_K if q_hi <= s < k_hi else None)
        if any(k is not None for k in kinds):
            table[j] = kinds
    return table


def _od_proj_kernel(x_ref, g_ref, w_ref, cos_ref, sin_ref, qkg_ref, o_ref, xn_ref, *, kinds_by_tile):
    j = pl.program_id(1)

    @pl.when(j == 0)
    def _():
        _norm_rows_to(x_ref, g_ref, xn_ref, x_ref.shape[0])

    acc = jnp.dot(xn_ref[...], w_ref[...], preferred_element_type=F32)

    plain = None
    for jj, kinds in kinds_by_tile.items():
        hit = j == jj
        plain = jnp.logical_not(hit) if plain is None else jnp.logical_and(plain, jnp.logical_not(hit))

        @pl.when(hit)
        def _(kinds=kinds):
            cos = cos_ref[...]
            sin = sin_ref[...]
            for c, kind in enumerate(kinds):
                blk = acc[:, c * LANES:(c + 1) * LANES]
                if kind == KIND_GQA_Q:
                    blk = _qk_norm_rope(blk, qkg_ref[0:1, :], cos, sin) * Q_PRESCALE
                elif kind == KIND_GQA_K:
                    blk = _qk_norm_rope(blk, qkg_ref[1:2, :], cos, sin)
                elif kind == KIND_DILATED_Q:
                    blk = blk * Q_PRESCALE
                o_ref[:, c * LANES:(c + 1) * LANES] = blk.astype(o_ref.dtype)

    @pl.when(plain)
    def _():
        o_ref[...] = acc.astype(o_ref.dtype)


def od_norm_proj(x, g, w, cos, sin_signed, qk_g, S):
    T, D = x.shape
    N = w.shape[1]
    tm, tn = PROJ_TM, PROJ_TN
    s_tiles = S // tm
    kernel = functools.partial(_od_proj_kernel, kinds_by_tile=_od_head_kinds(tn))
    return pl.pallas_call(
        kernel,
        grid=(T // tm, N // tn),
        in_specs=[
            pl.BlockSpec((tm, D), lambda i, j: (i, 0)),
            pl.BlockSpec((1, D), lambda i, j: (0, 0)),
            pl.BlockSpec((D, tn), lambda i, j: (0, j)),
            pl.BlockSpec((tm, HEAD_DIM), lambda i, j: (i % s_tiles, 0)),
            pl.BlockSpec((tm, HEAD_DIM), lambda i, j: (i % s_tiles, 0)),
            pl.BlockSpec((2, HEAD_DIM), lambda i, j: (0, 0)),
        ],
        out_specs=pl.BlockSpec((tm, tn), lambda i, j: (i, j)),
        out_shape=jax.ShapeDtypeStruct((T, N), BF16),
        scratch_shapes=[pltpu.VMEM((tm, D), BF16)],
        compiler_params=_params("parallel", "arbitrary"),
        name="od_norm_proj",
    )(x, g.reshape(1, D), w, cos, sin_signed, qk_g)


def _online_softmax(n, q_chunk, k_chunk, v_chunk, bias_chunk, n_chunks, row_sums):
    m = [None] * n
    acc = [None] * n
    den = [None] * n
    for j in range(n_chunks):
        for c in range(n):
            s = _nt_dot(q_chunk(c, j), k_chunk(c, j))
            bias = bias_chunk(c, j)
            if bias is not None:
                s = s + bias
            m_j = jnp.max(s, axis=-1, keepdims=True)
            m_new = m_j if j == 0 else jnp.maximum(m[c], m_j)
            p = jnp.exp2(s - m_new).astype(BF16)
            p_sum = jnp.sum(p.astype(F32), axis=-1, keepdims=True) if row_sums else None
            pv = jnp.dot(p, v_chunk(c, j), preferred_element_type=F32)
            if j == 0:
                acc[c] = pv
                den[c] = p_sum
            else:
                alpha = jnp.exp2(m[c] - m_new)
                acc[c] = alpha * acc[c] + pv
                if row_sums:
                    den[c] = alpha * den[c] + p_sum
            m[c] = m_new
    return acc, den


def _fill_value_and_ones(v_ref, va_ref):
    rows, d = v_ref.shape
    va_ref[:, :d] = v_ref[...]
    va_ref[:, d:] = jnp.ones((rows, va_ref.shape[1] - d), va_ref.dtype)


ALIBI_SPLIT = 3
ALIBI_POS_BITS = 6


def _bf16_terms(x):
    terms, rest = [], x
    for _ in range(ALIBI_SPLIT):
        t = rest.astype(BF16).astype(F32)
        terms.append(t)
        rest = rest - t
    return terms


def _alibi_features(pos, slope_terms, lane, key_side):
    hi = (pos >> ALIBI_POS_BITS).astype(F32)
    lo = (pos & ((1 << ALIBI_POS_BITS) - 1)).astype(F32)
    big = float(1 << ALIBI_POS_BITS)
    out = jnp.zeros(lane.shape, F32)
    for t, c in enumerate(slope_terms):
        if key_side:
            vals = (-big * c, -c, hi, lo)
        else:
            vals = (hi, lo, big * c, c)
        for group, v in enumerate(vals):
            out = jnp.where(lane == group * ALIBI_SPLIT + t, v, out)
    return out


def _diff_attn_kernel(slopes_ref, lam_ref, g_ref, q_ref, k_ref, v_ref, o_ref, bias_ref, kx_ref,
                      *, S, tq, n_sub, tk, lambda_init):
    h = pl.program_id(0)
    b = pl.program_id(1)
    i = pl.program_id(2)
    width = 2 * tk - tq
    chunk = min(tq, 2 * LANES)
    n_chunks = S // tk
    d = HEAD_DIM
    slope_terms = _bf16_terms(jnp.full((1, LANES), slopes_ref[h] * LOG2E, F32))

    @pl.when(i == 0)
    def _():
        for c in range(2):
            kx_ref[c, :, :d] = k_ref[:, c * d:(c + 1) * d]

    @pl.when(jnp.logical_and(b == 0, i == 0))
    def _():
        slope = slopes_ref[h] * LOG2E
        row = lax.broadcasted_iota(jnp.int32, (tq, chunk), 0)
        col = lax.broadcasted_iota(jnp.int32, (tq, chunk), 1)
        base = row + (tk - tq) - col

        def body(c, carry):
            off = pl.multiple_of(c * chunk, chunk)
            dist = jnp.abs(base - off).astype(F32)
            bias_ref[:, pl.ds(off, chunk)] = -(slope * dist)
            return carry

        lax.fori_loop(0, width // chunk, body, 0)

        def key_body(c, carry):
            off = pl.multiple_of(c * chunk, chunk)
            pos = off + lax.broadcasted_iota(jnp.int32, (chunk, LANES), 0)
            lane = lax.broadcasted_iota(jnp.int32, (chunk, LANES), 1)
            feat = _alibi_features(pos, slope_terms, lane, key_side=True).astype(BF16)
            kx_ref[0, pl.ds(off, chunk), d:] = feat
            kx_ref[1, pl.ds(off, chunk), d:] = feat
            return carry

        lax.fori_loop(0, S // chunk, key_body, 0)

    lv = lam_ref[...]
    lam = (jnp.exp(jnp.sum(lv[0:1] * lv[1:2], axis=-1, keepdims=True))
           - jnp.exp(jnp.sum(lv[2:3] * lv[3:4], axis=-1, keepdims=True)) + lambda_init)

    assert n_sub * tq == tk
    starts = [tk - tq - r * tq for r in range(n_sub)]
    lane = lax.broadcasted_iota(jnp.int32, (tq, LANES), 1)
    feats = []
    for r in range(n_sub):
        pos = (i * n_sub + r) * tq + lax.broadcasted_iota(jnp.int32, (tq, LANES), 0)
        feats.append(_alibi_features(pos, slope_terms, lane, key_side=False))
    zeros = jnp.zeros((tq, LANES), BF16)

    def key_chunk_index(role):
        cj = i + role
        return jnp.where(cj >= n_chunks, cj - n_chunks, cj)

    def q_chunk(n, role):
        r, c = n // 2, n % 2
        q = q_ref[r * tq:(r + 1) * tq, c * d:(c + 1) * d]
        if role == 0:
            return jnp.concatenate([q, zeros], axis=1)
        sign = jnp.where(i + role >= n_chunks, 1.0, -1.0)
        return jnp.concatenate([q, (feats[r] * sign).astype(BF16)], axis=1)

    def k_chunk(n, role):
        return kx_ref[n % 2, pl.ds(pl.multiple_of(key_chunk_index(role) * tk, tk), tk), :]

    def v_chunk(n, role):
        return v_ref[pl.ds(pl.multiple_of(key_chunk_index(role) * tk, tk), tk), :]

    def bias_chunk(n, role):
        return bias_ref[:, starts[n // 2]:starts[n // 2] + tk] if role == 0 else None

    acc, den = _online_softmax(2 * n_sub, q_chunk, k_chunk, v_chunk, bias_chunk, n_chunks, row_sums=True)
    for r in range(n_sub):
        o = acc[2 * r] / den[2 * r] - lam * (acc[2 * r + 1] / den[2 * r + 1])
        o = _rms(o, g_ref[...]) * (1.0 - lambda_init)
        o_ref[r * tq:(r + 1) * tq, :] = o.astype(o_ref.dtype)


def diff_attention(proj, lam_vec, subln_g, B, S, lambda_init):
    tq = ATTN_TQ
    dv = 2 * HEAD_DIM
    slopes = jnp.asarray([2.0 ** (-8.0 * (i + 1) / A_HEADS) for i in range(A_HEADS)], F32)
    n_sub = DENSE_Q_TILES
    tk = tq * n_sub
    kernel = functools.partial(_diff_attn_kernel, S=S, tq=tq, n_sub=n_sub, tk=tk, lambda_init=lambda_init)
    return pl.pallas_call(
        kernel,
        grid=(A_HEADS, B, S // (tq * n_sub)),
        in_specs=[
            pl.BlockSpec(memory_space=pltpu.SMEM),
            pl.BlockSpec((4, HEAD_DIM), lambda h, b, i: (0, 0)),
            pl.BlockSpec((1, dv), lambda h, b, i: (0, 0)),
            pl.BlockSpec((None, tq * n_sub, dv), lambda h, b, i: (b, i, h)),
            pl.BlockSpec((None, S, dv), lambda h, b, i: (b, 0, A_HEADS + h)),
            pl.BlockSpec((None, S, dv), lambda h, b, i: (b, 0, 2 * A_HEADS + h)),
        ],
        out_specs=pl.BlockSpec((None, tq * n_sub, dv), lambda h, b, i: (b, i, h)),
        out_shape=jax.ShapeDtypeStruct((B, S, A_W), BF16),
        scratch_shapes=[pltpu.VMEM((tq, 2 * tk - tq), F32), pltpu.VMEM((2, S, 2 * HEAD_DIM), BF16)],
        compiler_params=_params("arbitrary", "arbitrary", "arbitrary"),
        name="diff_attention",
    )(slopes, lam_vec.astype(F32), subln_g.reshape(1, dv), proj, proj, proj)


RPB_ROWS = 2 * NA_KH - 1
RPB_COLS = 2 * NA_KW - 1
NA_Q_ROWS = 4
NA_K_ROWS = 12
NA_Q = NA_Q_ROWS * GRID_W
NA_K = NA_K_ROWS * GRID_W
NA_GROUPS_PER_STEP = 4


def _na_case_rule(case, a):
    if case == 0:
        return 0, NA_KH, NA_KH - 1 - a
    if case == 1:
        return a, a + NA_KH, NA_KH // 2 - 1 - a
    return NA_K_ROWS - NA_KH, NA_K_ROWS, -1 - a


def _nbr_attn_kernel(rpb_ref, q_ref, k_ref, v_ref, o_ref, pair_ref, tab_ref, *, S):
    h = pl.program_id(0)
    b = pl.program_id(1)
    rows = S // GRID_W
    n_groups = rows // NA_Q_ROWS

    @pl.when(b == 0)
    def _():
        c = lax.broadcasted_iota(jnp.int32, (GRID_W, LANES), 0)
        lane = lax.broadcasted_iota(jnp.int32, (GRID_W, LANES), 1)
        w = lane & (GRID_W - 1)
        upper = lane >= GRID_W
        c0 = jnp.clip(c - NA_KW // 2, 0, GRID_W - NA_KW)
        valid = jnp.logical_and(w >= c0, w < c0 + NA_KW)
        dc = jnp.clip(w - c, 1 - NA_KW, NA_KW - 1) + NA_KW - 1
        base = h * (RPB_ROWS * RPB_COLS)
        for e in range(RPB_ROWS + 1):
            def body(d, acc, e=e):
                lo = rpb_ref[base + (e - 1) * RPB_COLS + d] * LOG2E if e >= 1 else 0.0
                hi = rpb_ref[base + e * RPB_COLS + d] * LOG2E if e < RPB_ROWS else 0.0
                return jnp.where(dc == d, jnp.where(upper, hi, lo), acc)

            acc = lax.fori_loop(0, RPB_COLS, body, jnp.zeros((GRID_W, LANES), F32))
            pair_ref[e] = jnp.where(valid, acc, NEG_INF)

        for case in range(3):
            for a in range(NA_Q_ROWS):
                lo_slot, hi_slot, off = _na_case_rule(case, a)
                for t in range(NA_K_ROWS // 2):
                    see0 = lo_slot <= 2 * t < hi_slot
                    see1 = lo_slot <= 2 * t + 1 < hi_slot
                    if see0 or see1:
                        tile = pair_ref[2 * t + off + 1]
                        if not see0:
                            tile = jnp.where(upper, tile, NEG_INF)
                        if not see1:
                            tile = jnp.where(upper, NEG_INF, tile)
                    else:
                        tile = jnp.full((GRID_W, LANES), NEG_INF, F32)
                    tab_ref[case, a * GRID_W:(a + 1) * GRID_W, t * LANES:(t + 1) * LANES] = tile

    def step(gp, carry):
        groups = [NA_GROUPS_PER_STEP * gp + a for a in range(NA_GROUPS_PER_STEP)]
        qs = [pl.multiple_of(gi * NA_Q, NA_Q) for gi in groups]
        ks = [pl.multiple_of(jnp.clip(gi * NA_Q_ROWS - NA_KH // 2, 0, rows - NA_K_ROWS) * GRID_W, GRID_W)
              for gi in groups]
        cases = [jnp.where(gi == 0, 0, jnp.where(gi == n_groups - 1, 2, 1)) for gi in groups]
        scores = [_nt_dot(q_ref[pl.ds(qs[a], NA_Q), :], k_ref[pl.ds(ks[a], NA_K), :]) + tab_ref[cases[a]]
                  for a in range(NA_GROUPS_PER_STEP)]
        ps = [jnp.exp2(s - jnp.max(s, axis=-1, keepdims=True)) for s in scores]
        ls = [jnp.sum(p, axis=-1, keepdims=True) for p in ps]
        for a in range(NA_GROUPS_PER_STEP):
            o = jnp.dot(ps[a].astype(BF16), v_ref[pl.ds(ks[a], NA_K), :], preferred_element_type=F32) / ls[a]
            o_ref[pl.ds(qs[a], NA_Q), :] = o.astype(o_ref.dtype)
        return carry

    lax.fori_loop(0, n_groups // NA_GROUPS_PER_STEP, step, 0)


def neighbourhood_attention(proj, rpb, B, S):
    q0 = 3 * A_W // HEAD_DIM
    kernel = functools.partial(_nbr_attn_kernel, S=S)
    return pl.pallas_call(
        kernel,
        grid=(B_HEADS, B),
        in_specs=[
            pl.BlockSpec(memory_space=pltpu.SMEM),
            pl.BlockSpec((None, S, HEAD_DIM), lambda h, b: (b, 0, q0 + h)),
            pl.BlockSpec((None, S, HEAD_DIM), lambda h, b: (b, 0, q0 + B_HEADS + h)),
            pl.BlockSpec((None, S, HEAD_DIM), lambda h, b: (b, 0, q0 + 2 * B_HEADS + h)),
        ],
        out_specs=pl.BlockSpec((None, S, HEAD_DIM), lambda h, b: (b, 0, h)),
        out_shape=jax.ShapeDtypeStruct((B, S, B_W), BF16),
        scratch_shapes=[pltpu.VMEM((RPB_ROWS + 1, GRID_W, LANES), F32),
                        pltpu.VMEM((3, NA_Q, NA_K), F32)],
        compiler_params=_params("arbitrary", "arbitrary"),
        name="neighbourhood_attention",
    )(rpb.astype(F32).reshape(-1), proj, proj, proj)


def _dil_geometry(S, tl, dil):
    half = DIL_RADIUS * dil
    hp = -(-half // LANES) * LANES
    kw = min(S, tl + 2 * hp)
    deltas = [min(max(l0 - hp, 0), S - kw) - l0 for l0 in range(0, S, tl)]
    u0 = -min(deltas)
    return half, hp, kw, u0, kw + max(deltas) + u0


DIL_LATTICE_GROUPS = (1, 2)


def _dilated_kernel(slopes_ref, q0_ref, q1_ref, q2_ref, k0_ref, k1_ref, k2_ref, v0_ref, v1_ref, v2_ref,
                    o_ref, *scratch, S, tl, n_sub, geoms):
    hh = pl.program_id(0)
    b = pl.program_id(1)
    i = pl.program_id(2)
    q_refs, k_refs, v_refs = (q0_ref, q1_ref, q2_ref), (k0_ref, k1_ref, k2_ref), (v0_ref, v1_ref, v2_ref)
    ng = len(C_PATTERNS)
    masked = [g for g in range(ng) if g not in DIL_LATTICE_GROUPS]
    bias_refs = dict(zip(masked, scratch[:len(masked)]))
    wide_ref, qf_ref = scratch[len(masked)], scratch[len(masked) + 1]
    rest = scratch[len(masked) + 2:]
    lattice = {g: rest[5 * n:5 * n + 5] for n, g in enumerate(DIL_LATTICE_GROUPS)}
    rows_step = tl * n_sub
    conv_chunk = 512

    @pl.when(i == 0)
    def _():
        for g in DIL_LATTICE_GROUPS:
            dil = C_PATTERNS[g][1]
            L = S // dil
            for src, dst in ((k_refs[g], lattice[g][0]), (v_refs[g], lattice[g][1])):
                def widen(c, carry, src=src):
                    r0 = pl.multiple_of(c * conv_chunk, conv_chunk)
                    wide_ref[pl.ds(r0, conv_chunk), :] = src[pl.ds(r0, conv_chunk), :].astype(F32)
                    return carry

                lax.fori_loop(0, S // conv_chunk, widen, 0)
                for r in range(dil):
                    dst[r * L:(r + 1) * L, :] = wide_ref[pl.ds(r, L, stride=dil), :].astype(BF16)

    @pl.when(jnp.logical_and(b == 0, i == 0))
    def _():
        for g in masked:
            dil = C_PATTERNS[g][1]
            half, _, _, u0, width = geoms[g]
            slope = slopes_ref[g * C_GROUP_HEADS + hh] * LOG2E
            row = lax.broadcasted_iota(jnp.int32, (tl, LANES), 0)
            col = lax.broadcasted_iota(jnp.int32, (tl, LANES), 1)
            base = col - row - u0

            def body(c, carry, g=g, dil=dil, half=half, slope=slope, base=base):
                off = pl.multiple_of(c * LANES, LANES)
                rel = base + off
                dist = jnp.abs(rel)
                ok = jnp.logical_and(dist <= half, (rel & (dil - 1)) == 0)
                bias_refs[g][:, pl.ds(off, LANES)] = jnp.where(ok, -(slope * dist.astype(F32)), NEG_INF)
                return carry

            lax.fori_loop(0, width // LANES, body, 0)

    gathered = {}
    for n, g in enumerate(DIL_LATTICE_GROUPS):
        dil = C_PATTERNS[g][1]
        nq = rows_step // dil
        qf_ref[n] = q_refs[g][...].astype(F32)
        gathered[g] = [qf_ref[n, pl.ds(r, nq, stride=dil), :].astype(BF16) for r in range(dil)]
    results = {}
    for g in DIL_LATTICE_GROUPS:
        dil = C_PATTERNS[g][1]
        L = S // dil
        nq = rows_step // dil
        kw = min(L, nq + 2 * DIL_RADIUS)
        kd_ref, vd_ref = lattice[g][:2]
        start = pl.multiple_of(jnp.clip(i * nq - DIL_RADIUS, 0, L - kw), DIL_RADIUS)
        mq = i * nq + lax.broadcasted_iota(jnp.int32, (nq, kw), 0)
        mk = start + lax.broadcasted_iota(jnp.int32, (nq, kw), 1)
        steps = jnp.abs(mk - mq)
        slope = slopes_ref[g * C_GROUP_HEADS + hh] * (LOG2E * dil)
        bias = jnp.where(steps <= DIL_RADIUS, -(slope * steps.astype(F32)), NEG_INF)
        ks = [pl.multiple_of(r * L + start, DIL_RADIUS) for r in range(dil)]
        scores = [_nt_dot(gathered[g][r], kd_ref[pl.ds(ks[r], kw), :]) + bias for r in range(dil)]
        ms = [jnp.max(s, axis=-1, keepdims=True) for s in scores]
        ps = [jnp.exp2(s - m) for s, m in zip(scores, ms)]
        ls = [jnp.sum(p, axis=-1, keepdims=True) for p in ps]
        accs = [jnp.dot(ps[r].astype(BF16), vd_ref[pl.ds(ks[r], kw), :], preferred_element_type=F32)
                for r in range(dil)]
        results[g] = list(zip(accs, ms, ls))
    for g in DIL_LATTICE_GROUPS:
        dil = C_PATTERNS[g][1]
        nq = rows_step // dil
        acc_ref, m_ref, l_ref = lattice[g][2:]
        for r, (acc, m, l) in enumerate(results[g]):
            acc_ref[pl.ds(r, nq, stride=dil), :] = acc
            m_ref[pl.ds(r, nq, stride=dil), :] = jnp.broadcast_to(m, (nq, HEAD_DIM))
            l_ref[pl.ds(r, nq, stride=dil), :] = jnp.broadcast_to(l, (nq, HEAD_DIM))

    for r in range(n_sub):
        rows = slice(r * tl, (r + 1) * tl)
        l0 = (i * n_sub + r) * tl
        accs, ms, ls = [], [], []
        for g in range(ng):
            if g in lattice:
                _, _, acc_ref, m_ref, l_ref = lattice[g]
                accs.append(acc_ref[rows, :])
                ms.append(m_ref[rows, :])
                ls.append(l_ref[rows, :])
                continue
            _, hp, kw, u0, _ = geoms[g]
            ws = pl.multiple_of(jnp.clip(l0 - hp, 0, S - kw), LANES)
            start = pl.multiple_of(ws - l0 + u0, LANES)
            k = k_refs[g][pl.ds(ws, kw), :]
            v = v_refs[g][pl.ds(ws, kw), :]
            s = _nt_dot(q_refs[g][rows, :], k) + bias_refs[g][:, pl.ds(start, kw)]
            m = jnp.max(s, axis=-1, keepdims=True)
            p = jnp.exp2(s - m)
            ls.append(jnp.sum(p, axis=-1, keepdims=True))
            ms.append(m)
            accs.append(jnp.dot(p.astype(BF16), v, preferred_element_type=F32))

        m_all = jnp.maximum(jnp.maximum(ms[0], ms[1]), ms[2])
        ws_ = [jnp.exp2(m - m_all) for m in ms]
        num = ws_[0] * accs[0] + ws_[1] * accs[1] + ws_[2] * accs[2]
        den = ws_[0] * ls[0] + ws_[1] * ls[1] + ws_[2] * ls[2]
        o_ref[rows, :] = (num / den).astype(o_ref.dtype)


def dilated_mixture_attention(proj, B, S):
    tl = ATTN_TQ
    ng = len(C_PATTERNS)
    for win, dil in C_PATTERNS:
        assert win // (2 * dil) == DIL_RADIUS and dil & (dil - 1) == 0
    geoms = tuple(_dil_geometry(S, tl, dil) for _, dil in C_PATTERNS)
    slopes = jnp.asarray([2.0 ** (-8.0 * (i + 1) / C_HEADS) for i in range(C_HEADS)], F32)
    k0, v0 = C_W // HEAD_DIM, 2 * C_W // HEAD_DIM

    n_sub = DENSE_Q_TILES
    rows_step = tl * n_sub

    def q_spec(g):
        return pl.BlockSpec((None, rows_step, HEAD_DIM), lambda hh, b, i: (b, i, g * C_GROUP_HEADS + hh))

    def kv_spec(first, g):
        return pl.BlockSpec((None, S, HEAD_DIM), lambda hh, b, i: (b, 0, first + g * C_GROUP_HEADS + hh))

    masked = [g for g in range(ng) if g not in DIL_LATTICE_GROUPS]
    scratch = [pltpu.VMEM((tl, geoms[g][4]), F32) for g in masked]
    scratch += [pltpu.VMEM((S, HEAD_DIM), F32), pltpu.VMEM((len(DIL_LATTICE_GROUPS), rows_step, HEAD_DIM), F32)]
    for g in DIL_LATTICE_GROUPS:
        assert rows_step % (C_PATTERNS[g][1] * 8) == 0
        scratch += [pltpu.VMEM((S, HEAD_DIM), BF16)] * 2 + [pltpu.VMEM((rows_step, HEAD_DIM), F32)] * 3

    kernel = functools.partial(_dilated_kernel, S=S, tl=tl, n_sub=n_sub, geoms=geoms)
    return pl.pallas_call(
        kernel,
        grid=(C_GROUP_HEADS, B, S // rows_step),
        in_specs=([pl.BlockSpec(memory_space=pltpu.SMEM)]
                  + [q_spec(g) for g in range(ng)]
                  + [kv_spec(k0, g) for g in range(ng)]
                  + [kv_spec(v0, g) for g in range(ng)]),
        out_specs=pl.BlockSpec((None, rows_step, HEAD_DIM), lambda hh, b, i: (b, i, hh)),
        out_shape=jax.ShapeDtypeStruct((B, S, OC_W), BF16),
        scratch_shapes=scratch,
        compiler_params=_params("arbitrary", "arbitrary", "arbitrary"),
        name="dilated_mixture_attention",
    )(slopes, *([proj] * (3 * ng)))


def _gqa_kernel(q_ref, k_ref, v_ref, o_ref, va_ref, *, S, tq, n_sub, tk):
    d = HEAD_DIM

    @pl.when(pl.program_id(2) == 0)
    def _():
        _fill_value_and_ones(v_ref, va_ref)

    chains = [(r, g) for r in range(n_sub) for g in range(D_GROUP)]
    acc, _ = _online_softmax(
        len(chains),
        lambda n, j: q_ref[chains[n][0] * tq:(chains[n][0] + 1) * tq, chains[n][1] * d:(chains[n][1] + 1) * d],
        lambda n, j: k_ref[j * tk:(j + 1) * tk, :],
        lambda n, j: va_ref[j * tk:(j + 1) * tk, :],
        lambda n, j: None,
        S // tk, row_sums=False)
    for n, (r, g) in enumerate(chains):
        o_ref[r * tq:(r + 1) * tq, g * d:(g + 1) * d] = (acc[n][:, :d] / acc[n][:, d:]).astype(o_ref.dtype)


def gqa_attention(proj, B, S):
    tq = ATTN_TQ
    qw = D_GROUP * HEAD_DIM
    q0 = 3 * C_W // qw
    k0 = (3 * C_W + DQ_W) // HEAD_DIM
    v0 = k0 + D_KV_HEADS
    n_sub = DENSE_Q_TILES
    tk = min(GQA_TK, S)
    return pl.pallas_call(
        functools.partial(_gqa_kernel, S=S, tq=tq, n_sub=n_sub, tk=tk),
        grid=(D_KV_HEADS, B, S // (tq * n_sub)),
        in_specs=[
            pl.BlockSpec((None, tq * n_sub, qw), lambda h, b, i: (b, i, q0 + h)),
            pl.BlockSpec((None, S, HEAD_DIM), lambda h, b, i: (b, 0, k0 + h)),
            pl.BlockSpec((None, S, HEAD_DIM), lambda h, b, i: (b, 0, v0 + h)),
        ],
        out_specs=pl.BlockSpec((None, tq * n_sub, qw), lambda h, b, i: (b, i, h)),
        out_shape=jax.ShapeDtypeStruct((B, S, DQ_W), BF16),
        scratch_shapes=[pltpu.VMEM((S, 2 * HEAD_DIM), BF16)],
        compiler_params=_params("arbitrary", "arbitrary", "arbitrary"),
        name="gqa_attention",
    )(proj, proj, proj)


def _out_proj_kernel(a1_ref, a2_ref, w1_ref, w2_ref, h_ref, o_ref):
    acc = jnp.dot(a1_ref[...], w1_ref[...], preferred_element_type=F32)
    acc = acc + jnp.dot(a2_ref[...], w2_ref[...], preferred_element_type=F32)
    o_ref[...] = h_ref[...] + acc


def out_proj_residual(a1, a2, w1, w2, h):
    T, D = h.shape
    k1, k2 = a1.shape[1], a2.shape[1]
    tm = OUT_PROJ_TM
    resident = pl.Buffered(1)
    return pl.pallas_call(
        _out_proj_kernel,
        grid=(T // tm,),
        in_specs=[
            pl.BlockSpec((tm, k1), lambda i: (i, 0)),
            pl.BlockSpec((tm, k2), lambda i: (i, 0)),
            pl.BlockSpec((k1, D), lambda i: (0, 0), pipeline_mode=resident),
            pl.BlockSpec((k2, D), lambda i: (0, 0), pipeline_mode=resident),
            pl.BlockSpec((tm, D), lambda i: (i, 0)),
        ],
        out_specs=pl.BlockSpec((tm, D), lambda i: (i, 0)),
        out_shape=jax.ShapeDtypeStruct((T, D), F32),
        compiler_params=_params("parallel"),
        name="out_proj_residual",
    )(a1, a2, w1, w2, h)


def _ffn_kernel(x_ref, g_ref, wg_ref, wu_ref, wd_ref, fg_ref, o_ref, xn_ref, *, final_norm):
    k = pl.program_id(1)

    @pl.when(k == 0)
    def _():
        _norm_rows_to(x_ref, g_ref, xn_ref, x_ref.shape[0])
        o_ref[...] = x_ref[...]

    xn = xn_ref[...]
    gate = jnp.dot(xn, wg_ref[...], preferred_element_type=F32)
    up = jnp.dot(xn, wu_ref[...], preferred_element_type=F32)
    act = (gate * jax.nn.sigmoid(gate)) * up
    o_ref[...] += jnp.dot(act.astype(BF16), wd_ref[...], preferred_element_type=F32)

    if final_norm:
        @pl.when(k == pl.num_programs(1) - 1)
        def _():
            fg = fg_ref[...]

            def body(c, carry):
                r = pl.multiple_of(c * NORM_ROW_CHUNK, NORM_ROW_CHUNK)
                o_ref[pl.ds(r, NORM_ROW_CHUNK), :] = _rms(o_ref[pl.ds(r, NORM_ROW_CHUNK), :], fg)
                return carry

            lax.fori_loop(0, o_ref.shape[0] // NORM_ROW_CHUNK, body, 0)


def ffn_residual(x, g, wg, wu, wd, final_g, final_norm):
    T, D = x.shape
    H = wg.shape[1]
    tm, th = FFN_TM, FFN_TH
    kernel = functools.partial(_ffn_kernel, final_norm=final_norm)
    return pl.pallas_call(
        kernel,
        grid=(T // tm, H // th),
        in_specs=[
            pl.BlockSpec((tm, D), lambda i, k: (i, 0)),
            pl.BlockSpec((1, D), lambda i, k: (0, 0)),
            pl.BlockSpec((D, th), lambda i, k: (0, k)),
            pl.BlockSpec((D, th), lambda i, k: (0, k)),
            pl.BlockSpec((th, D), lambda i, k: (k, 0)),
            pl.BlockSpec((1, D), lambda i, k: (0, 0)),
        ],
        out_specs=pl.BlockSpec((tm, D), lambda i, k: (i, 0)),
        out_shape=jax.ShapeDtypeStruct((T, D), F32),
        scratch_shapes=[pltpu.VMEM((tm, D), BF16)],
        compiler_params=_params("parallel", "arbitrary"),
        name="ffn_residual",
    )(x, g.reshape(1, D), wg, wu, wd, final_g.reshape(1, D))


def _rope_tables(S):
    t = jnp.arange(S)
    row = (t // GRID_W).astype(F32)
    col = (t % GRID_W).astype(F32)
    half = HEAD_DIM // 2
    f_row = ROPE_THETA ** (-jnp.arange(0, half, 2, dtype=F32) / half)
    f_col = ROPE_THETA ** (-jnp.arange(0, HEAD_DIM - half, 2, dtype=F32) / (HEAD_DIM - half))
    ang = jnp.concatenate([row[:, None] * f_row[None, :], col[:, None] * f_col[None, :]], axis=-1)
    cos = jnp.repeat(jnp.cos(ang), 2, axis=-1)
    sin = jnp.repeat(jnp.sin(ang), 2, axis=-1)
    sign = jnp.where(jnp.arange(HEAD_DIM) % 2 == 0, -1.0, 1.0).astype(F32)
    return cos, sin * sign[None, :]


def _lambda_init(layer_idx):
    return 0.8 - 0.6 * math.exp(-0.3 * layer_idx)


def _trunk(x, w):
    B, S, D = x.shape
    T = B * S
    h = x.reshape(T, D)
    cos, sin_signed = _rope_tables(S)
    for i in range(DEPTH):
        j = i // 2
        if i % 2 == 0:
            proj = norm_proj(h, w["attn_norm_g"][i], w["ev_w_in"][j])
            proj3 = proj.reshape(B, S, EV_IN)
            oa = diff_attention(proj3, w["ev_lambda"][j], w["ev_subln_g"][j], B, S, _lambda_init(i))
            ob = neighbourhood_attention(proj3, w["ev_rpb"][j], B, S)
            h = out_proj_residual(oa.reshape(T, A_W), ob.reshape(T, B_W),
                                  w["ev_w_out"][j][0], w["ev_w_out"][j][1], h)
        else:
            proj = od_norm_proj(h, w["attn_norm_g"][i], w["od_w_in"][j], cos, sin_signed,
                                w["od_qk_norm_g"][j].astype(F32), S)
            proj3 = proj.reshape(B, S, OD_IN)
            oc = dilated_mixture_attention(proj3, B, S).reshape(T, OC_W)
            od = gqa_attention(proj3, B, S)
            h = out_proj_residual(oc, od.reshape(T, DQ_W), w["od_w_out"][j][0], w["od_w_out"][j][1], h)
        h = ffn_residual(h, w["ffn_norm_g"][i], w["ffn_w_gate"][i], w["ffn_w_up"][i], w["ffn_w_down"][i],
                         w["final_norm_g"], final_norm=(i == DEPTH - 1))
    return h.reshape(B, S, D)


def kernel(x_prompt, x_sample, attn_norm_g, ev_w_in, ev_lambda, ev_subln_g, ev_rpb, ev_w_out, od_w_in,
           od_qk_norm_g, od_w_out, ffn_norm_g, ffn_w_gate, ffn_w_up, ffn_w_down, final_norm_g):
    w = {
        "attn_norm_g": attn_norm_g, "ffn_norm_g": ffn_norm_g, "final_norm_g": final_norm_g,
        "ev_lambda": ev_lambda, "ev_subln_g": ev_subln_g, "ev_rpb": ev_rpb, "od_qk_norm_g": od_qk_norm_g,
        "ev_w_in": [ev_w_in[j].astype(BF16) for j in range(ev_w_in.shape[0])],
        "ev_w_out": [(ev_w_out[j, :A_W].astype(BF16), ev_w_out[j, A_W:].astype(BF16))
                     for j in range(ev_w_out.shape[0])],
        "od_w_in": [od_w_in[j].astype(BF16) for j in range(od_w_in.shape[0])],
        "od_w_out": [(od_w_out[j, :OC_W].astype(BF16), od_w_out[j, OC_W:].astype(BF16))
                     for j in range(od_w_out.shape[0])],
        "ffn_w_gate": [ffn_w_gate[i].astype(BF16) for i in range(DEPTH)],
        "ffn_w_up": [ffn_w_up[i].astype(BF16) for i in range(DEPTH)],
        "ffn_w_down": [ffn_w_down[i].astype(BF16) for i in range(DEPTH)],
    }
    return _trunk(x_prompt, w), _trunk(x_sample, w)
```

```python
import functools
import math

import jax
import jax.numpy as jnp
from jax import lax
from jax.experimental import pallas as pl
from jax.experimental.pallas import tpu as pltpu

F32 = jnp.float32
BF16 = jnp.bfloat16

D_MODEL = 2048
DEPTH = 2
HEAD_DIM = 128
GRID_W = 64
EPS = 1e-6
NEG_INF = -1e30
SCALE = HEAD_DIM ** -0.5
LOG2E = math.log2(math.e)
Q_PRESCALE = SCALE * LOG2E

A_HEADS = 4
B_HEADS = 8
NA_KH = 8
NA_KW = 16
C_PATTERNS = ((128, 1), (512, 4), (2048, 16))
C_GROUP_HEADS = 4
C_HEADS = 12
D_HEADS = 12
D_KV_HEADS = 4
D_GROUP = D_HEADS // D_KV_HEADS
ROPE_THETA = 10000.0
FFN_HIDDEN = 5632

A_W = A_HEADS * 2 * HEAD_DIM
B_W = B_HEADS * HEAD_DIM
EV_IN = 3 * A_W + 3 * B_W
C_W = C_HEADS * HEAD_DIM
DQ_W = D_HEADS * HEAD_DIM
DKV_W = D_KV_HEADS * HEAD_DIM
OD_IN = 3 * C_W + DQ_W + 2 * DKV_W
OC_W = C_GROUP_HEADS * HEAD_DIM

VMEM_LIMIT_BYTES = 56 * 1024 * 1024
LANES = 128

PROJ_TM = 1024
PROJ_TN = 1024
FFN_TM = 1024
FFN_TH = 512
NORM_ROW_CHUNK = 256
ATTN_TQ = 256
DENSE_Q_TILES = 4
GQA_TK = 512
DIL_RADIUS = 64
OUT_PROJ_TM = 512


def _params(*sem):
    return pltpu.CompilerParams(dimension_semantics=sem, vmem_limit_bytes=VMEM_LIMIT_BYTES)


def _nt_dot(a, b):
    return lax.dot_general(a, b, (((1,), (1,)), ((), ())), preferred_element_type=F32)


def _rms(x, g):
    ms = jnp.mean(x * x, axis=-1, keepdims=True)
    return (x * lax.rsqrt(ms + EPS)) * g


def _norm_rows_to(x_ref, g_ref, xn_ref, rows):
    g = g_ref[...]

    def body(c, carry):
        r = pl.multiple_of(c * NORM_ROW_CHUNK, NORM_ROW_CHUNK)
        xn_ref[pl.ds(r, NORM_ROW_CHUNK), :] = _rms(x_ref[pl.ds(r, NORM_ROW_CHUNK), :], g).astype(BF16)
        return carry

    lax.fori_loop(0, rows // NORM_ROW_CHUNK, body, 0)


def _norm_proj_kernel(x_ref, g_ref, w_ref, o_ref, xn_ref, *, q_tiles):
    j = pl.program_id(1)

    @pl.when(j == 0)
    def _():
        _norm_rows_to(x_ref, g_ref, xn_ref, x_ref.shape[0])

    is_q = functools.reduce(jnp.logical_or, [j == jj for jj in q_tiles])
    factor = jnp.where(is_q, Q_PRESCALE, 1.0).astype(F32)
    acc = jnp.dot(xn_ref[...], w_ref[...], preferred_element_type=F32)
    o_ref[...] = (acc * factor).astype(o_ref.dtype)


def norm_proj(x, g, w):
    T, D = x.shape
    N = w.shape[1]
    tm, tn = PROJ_TM, PROJ_TN
    assert A_W % tn == 0 and B_W % tn == 0
    q_tiles = tuple(range(A_W // tn)) + tuple(range(3 * A_W // tn, (3 * A_W + B_W) // tn))
    return pl.pallas_call(
        functools.partial(_norm_proj_kernel, q_tiles=q_tiles),
        grid=(T // tm, N // tn),
        in_specs=[
            pl.BlockSpec((tm, D), lambda i, j: (i, 0)),
            pl.BlockSpec((1, D), lambda i, j: (0, 0)),
            pl.BlockSpec((D, tn), lambda i, j: (0, j)),
        ],
        out_specs=pl.BlockSpec((tm, tn), lambda i, j: (i, j)),
        out_shape=jax.ShapeDtypeStruct((T, N), BF16),
        scratch_shapes=[pltpu.VMEM((tm, D), BF16)],
        compiler_params=_params("parallel", "arbitrary"),
        name="norm_proj",
    )(x, g.reshape(1, D), w)


def _qk_norm_rope(x, g, cos, sin_signed):
    y = _rms(x, g)
    lane = lax.broadcasted_iota(jnp.int32, y.shape, 1)
    nxt = pltpu.roll(y, LANES - 1, 1)
    prv = pltpu.roll(y, 1, 1)
    partner = jnp.where((lane & 1) == 0, nxt, prv)
    return y * cos + partner * sin_signed


KIND_DILATED_Q, KIND_GQA_Q, KIND_GQA_K = "dilated_q", "gqa_q", "gqa_k"


def _od_head_kinds(tn):
    slots = tn // LANES
    c_hi = C_W // LANES
    q_lo, q_hi = 3 * C_W // LANES, (3 * C_W + DQ_W) // LANES
    k_hi = q_hi + DKV_W // LANES
    table = {}
    for j in range(OD_IN // tn):
        kinds = []
        for c in range(slots):
            s = j * slots + c
            kinds.append(KIND_DILATED_Q if s < c_hi else KIND_GQA_Q if q_lo <= s < q_hi
                         else KIND_GQA_K if q_hi <= s < k_hi else None)
        if any(k is not None for k in kinds):
            table[j] = kinds
    return table


def _od_proj_kernel(x_ref, g_ref, w_ref, cos_ref, sin_ref, qkg_ref, o_ref, xn_ref, *, kinds_by_tile):
    j = pl.program_id(1)

    @pl.when(j == 0)
    def _():
        _norm_rows_to(x_ref, g_ref, xn_ref, x_ref.shape[0])

    acc = jnp.dot(xn_ref[...], w_ref[...], preferred_element_type=F32)

    plain = None
    for jj, kinds in kinds_by_tile.items():
        hit = j == jj
        plain = jnp.logical_not(hit) if plain is None else jnp.logical_and(plain, jnp.logical_not(hit))

        @pl.when(hit)
        def _(kinds=kinds):
            cos = cos_ref[...]
            sin = sin_ref[...]
            for c, kind in enumerate(kinds):
                blk = acc[:, c * LANES:(c + 1) * LANES]
                if kind == KIND_GQA_Q:
                    blk = _qk_norm_rope(blk, qkg_ref[0:1, :], cos, sin) * Q_PRESCALE
                elif kind == KIND_GQA_K:
                    blk = _qk_norm_rope(blk, qkg_ref[1:2, :], cos, sin)
                elif kind == KIND_DILATED_Q:
                    blk = blk * Q_PRESCALE
                o_ref[:, c * LANES:(c + 1) * LANES] = blk.astype(o_ref.dtype)

    @pl.when(plain)
    def _():
        o_ref[...] = acc.astype(o_ref.dtype)


def od_norm_proj(x, g, w, cos, sin_signed, qk_g, S):
    T, D = x.shape
    N = w.shape[1]
    tm, tn = PROJ_TM, PROJ_TN
    s_tiles = S // tm
    kernel = functools.partial(_od_proj_kernel, kinds_by_tile=_od_head_kinds(tn))
    return pl.pallas_call(
        kernel,
        grid=(T // tm, N // tn),
        in_specs=[
            pl.BlockSpec((tm, D), lambda i, j: (i, 0)),
            pl.BlockSpec((1, D), lambda i, j: (0, 0)),
            pl.BlockSpec((D, tn), lambda i, j: (0, j)),
            pl.BlockSpec((tm, HEAD_DIM), lambda i, j: (i % s_tiles, 0)),
            pl.BlockSpec((tm, HEAD_DIM), lambda i, j: (i % s_tiles, 0)),
            pl.BlockSpec((2, HEAD_DIM), lambda i, j: (0, 0)),
        ],
        out_specs=pl.BlockSpec((tm, tn), lambda i, j: (i, j)),
        out_shape=jax.ShapeDtypeStruct((T, N), BF16),
        scratch_shapes=[pltpu.VMEM((tm, D), BF16)],
        compiler_params=_params("parallel", "arbitrary"),
        name="od_norm_proj",
    )(x, g.reshape(1, D), w, cos, sin_signed, qk_g)


def _online_softmax(n, q_chunk, k_chunk, v_chunk, bias_chunk, n_chunks, row_sums):
    m = [None] * n
    acc = [None] * n
    den = [None] * n
    for j in range(n_chunks):
        for c in range(n):
            s = _nt_dot(q_chunk(c, j), k_chunk(c, j))
            bias = bias_chunk(c, j)
            if bias is not None:
                s = s + bias
            m_j = jnp.max(s, axis=-1, keepdims=True)
            m_new = m_j if j == 0 else jnp.maximum(m[c], m_j)
            p = jnp.exp2(s - m_new).astype(BF16)
            p_sum = jnp.sum(p.astype(F32), axis=-1, keepdims=True) if row_sums else None
            pv = jnp.dot(p, v_chunk(c, j), preferred_element_type=F32)
            if j == 0:
                acc[c] = pv
                den[c] = p_sum
            else:
                alpha = jnp.exp2(m[c] - m_new)
                acc[c] = alpha * acc[c] + pv
                if row_sums:
                    den[c] = alpha * den[c] + p_sum
            m[c] = m_new
    return acc, den


def _fill_value_and_ones(v_ref, va_ref):
    rows, d = v_ref.shape
    va_ref[:, :d] = v_ref[...]
    va_ref[:, d:] = jnp.ones((rows, va_ref.shape[1] - d), va_ref.dtype)


ALIBI_SPLIT = 3
ALIBI_POS_BITS = 6


def _bf16_terms(x):
    terms, rest = [], x
    for _ in range(ALIBI_SPLIT):
        t = rest.astype(BF16).astype(F32)
        terms.append(t)
        rest = rest - t
    return terms


def _alibi_features(pos, slope_terms, lane, key_side):
    hi = (pos >> ALIBI_POS_BITS).astype(F32)
    lo = (pos & ((1 << ALIBI_POS_BITS) - 1)).astype(F32)
    big = float(1 << ALIBI_POS_BITS)
    out = jnp.zeros(lane.shape, F32)
    for t, c in enumerate(slope_terms):
        if key_side:
            vals = (-big * c, -c, hi, lo)
        else:
            vals = (hi, lo, big * c, c)
        for group, v in enumerate(vals):
            out = jnp.where(lane == group * ALIBI_SPLIT + t, v, out)
    return out


def _diff_attn_kernel(slopes_ref, lam_ref, g_ref, q_ref, k_ref, v_ref, o_ref, bias_ref, kx_ref,
                      *, S, tq, n_sub, tk, lambda_init):
    h = pl.program_id(0)
    b = pl.program_id(1)
    i = pl.program_id(2)
    width = 2 * tk - tq
    chunk = min(tq, 2 * LANES)
    n_chunks = S // tk
    d = HEAD_DIM
    slope_terms = _bf16_terms(jnp.full((1, LANES), slopes_ref[h] * LOG2E, F32))

    @pl.when(i == 0)
    def _():
        for c in range(2):
            kx_ref[c, :, :d] = k_ref[:, c * d:(c + 1) * d]

    @pl.when(jnp.logical_and(b == 0, i == 0))
    def _():
        slope = slopes_ref[h] * LOG2E
        row = lax.broadcasted_iota(jnp.int32, (tq, chunk), 0)
        col = lax.broadcasted_iota(jnp.int32, (tq, chunk), 1)
        base = row + (tk - tq) - col

        def body(c, carry):
            off = pl.multiple_of(c * chunk, chunk)
            dist = jnp.abs(base - off).astype(F32)
            bias_ref[:, pl.ds(off, chunk)] = -(slope * dist)
            return carry

        lax.fori_loop(0, width // chunk, body, 0)

        def key_body(c, carry):
            off = pl.multiple_of(c * chunk, chunk)
            pos = off + lax.broadcasted_iota(jnp.int32, (chunk, LANES), 0)
            lane = lax.broadcasted_iota(jnp.int32, (chunk, LANES), 1)
            feat = _alibi_features(pos, slope_terms, lane, key_side=True).astype(BF16)
            kx_ref[0, pl.ds(off, chunk), d:] = feat
            kx_ref[1, pl.ds(off, chunk), d:] = feat
            return carry

        lax.fori_loop(0, S // chunk, key_body, 0)

    lv = lam_ref[...]
    lam = (jnp.exp(jnp.sum(lv[0:1] * lv[1:2], axis=-1, keepdims=True))
           - jnp.exp(jnp.sum(lv[2:3] * lv[3:4], axis=-1, keepdims=True)) + lambda_init)

    assert n_sub * tq == tk
    starts = [tk - tq - r * tq for r in range(n_sub)]
    lane = lax.broadcasted_iota(jnp.int32, (tq, LANES), 1)
    feats = []
    for r in range(n_sub):
        pos = (i * n_sub + r) * tq + lax.broadcasted_iota(jnp.int32, (tq, LANES), 0)
        feats.append(_alibi_features(pos, slope_terms, lane, key_side=False))
    zeros = jnp.zeros((tq, LANES), BF16)

    def key_chunk_index(role):
        cj = i + role
        return jnp.where(cj >= n_chunks, cj - n_chunks, cj)

    def q_chunk(n, role):
        r, c = n // 2, n % 2
        q = q_ref[r * tq:(r + 1) * tq, c * d:(c + 1) * d]
        if role == 0:
            return jnp.concatenate([q, zeros], axis=1)
        sign = jnp.where(i + role >= n_chunks, 1.0, -1.0)
        return jnp.concatenate([q, (feats[r] * sign).astype(BF16)], axis=1)

    def k_chunk(n, role):
        return kx_ref[n % 2, pl.ds(pl.multiple_of(key_chunk_index(role) * tk, tk), tk), :]

    def v_chunk(n, role):
        return v_ref[pl.ds(pl.multiple_of(key_chunk_index(role) * tk, tk), tk), :]

    def bias_chunk(n, role):
        return bias_ref[:, starts[n // 2]:starts[n // 2] + tk] if role == 0 else None

    acc, den = _online_softmax(2 * n_sub, q_chunk, k_chunk, v_chunk, bias_chunk, n_chunks, row_sums=True)
    for r in range(n_sub):
        o = acc[2 * r] / den[2 * r] - lam * (acc[2 * r + 1] / den[2 * r + 1])
        o = _rms(o, g_ref[...]) * (1.0 - lambda_init)
        o_ref[r * tq:(r + 1) * tq, :] = o.astype(o_ref.dtype)


def diff_attention(proj, lam_vec, subln_g, B, S, lambda_init):
    tq = ATTN_TQ
    dv = 2 * HEAD_DIM
    slopes = jnp.asarray([2.0 ** (-8.0 * (i + 1) / A_HEADS) for i in range(A_HEADS)], F32)
    n_sub = DENSE_Q_TILES
    tk = tq * n_sub
    kernel = functools.partial(_diff_attn_kernel, S=S, tq=tq, n_sub=n_sub, tk=tk, lambda_init=lambda_init)
    return pl.pallas_call(
        kernel,
        grid=(A_HEADS, B, S // (tq * n_sub)),
        in_specs=[
            pl.BlockSpec(memory_space=pltpu.SMEM),
            pl.BlockSpec((4, HEAD_DIM), lambda h, b, i: (0, 0)),
            pl.BlockSpec((1, dv), lambda h, b, i: (0, 0)),
            pl.BlockSpec((None, tq * n_sub, dv), lambda h, b, i: (b, i, h)),
            pl.BlockSpec((None, S, dv), lambda h, b, i: (b, 0, A_HEADS + h)),
            pl.BlockSpec((None, S, dv), lambda h, b, i: (b, 0, 2 * A_HEADS + h)),
        ],
        out_specs=pl.BlockSpec((None, tq * n_sub, dv), lambda h, b, i: (b, i, h)),
        out_shape=jax.ShapeDtypeStruct((B, S, A_W), BF16),
        scratch_shapes=[pltpu.VMEM((tq, 2 * tk - tq), F32), pltpu.VMEM((2, S, 2 * HEAD_DIM), BF16)],
        compiler_params=_params("arbitrary", "arbitrary", "arbitrary"),
        name="diff_attention",
    )(slopes, lam_vec.astype(F32), subln_g.reshape(1, dv), proj, proj, proj)


RPB_ROWS = 2 * NA_KH - 1
RPB_COLS = 2 * NA_KW - 1
NA_Q_ROWS = 4
NA_K_ROWS = 12
NA_Q = NA_Q_ROWS * GRID_W
NA_K = NA_K_ROWS * GRID_W
NA_GROUPS_PER_STEP = 4


def _na_case_rule(case, a):
    if case == 0:
        return 0, NA_KH, NA_KH - 1 - a
    if case == 1:
        return a, a + NA_KH, NA_KH // 2 - 1 - a
    return NA_K_ROWS - NA_KH, NA_K_ROWS, -1 - a


def _nbr_attn_kernel(rpb_ref, q_ref, k_ref, v_ref, o_ref, pair_ref, tab_ref, *, S):
    h = pl.program_id(0)
    b = pl.program_id(1)
    rows = S // GRID_W
    n_groups = rows // NA_Q_ROWS

    @pl.when(b == 0)
    def _():
        c = lax.broadcasted_iota(jnp.int32, (GRID_W, LANES), 0)
        lane = lax.broadcasted_iota(jnp.int32, (GRID_W, LANES), 1)
        w = lane & (GRID_W - 1)
        upper = lane >= GRID_W
        c0 = jnp.clip(c - NA_KW // 2, 0, GRID_W - NA_KW)
        valid = jnp.logical_and(w >= c0, w < c0 + NA_KW)
        dc = jnp.clip(w - c, 1 - NA_KW, NA_KW - 1) + NA_KW - 1
        base = h * (RPB_ROWS * RPB_COLS)
        for e in range(RPB_ROWS + 1):
            def body(d, acc, e=e):
                lo = rpb_ref[base + (e - 1) * RPB_COLS + d] * LOG2E if e >= 1 else 0.0
                hi = rpb_ref[base + e * RPB_COLS + d] * LOG2E if e < RPB_ROWS else 0.0
                return jnp.where(dc == d, jnp.where(upper, hi, lo), acc)

            acc = lax.fori_loop(0, RPB_COLS, body, jnp.zeros((GRID_W, LANES), F32))
            pair_ref[e] = jnp.where(valid, acc, NEG_INF)

        for case in range(3):
            for a in range(NA_Q_ROWS):
                lo_slot, hi_slot, off = _na_case_rule(case, a)
                for t in range(NA_K_ROWS // 2):
                    see0 = lo_slot <= 2 * t < hi_slot
                    see1 = lo_slot <= 2 * t + 1 < hi_slot
                    if see0 or see1:
                        tile = pair_ref[2 * t + off + 1]
                        if not see0:
                            tile = jnp.where(upper, tile, NEG_INF)
                        if not see1:
                            tile = jnp.where(upper, NEG_INF, tile)
                    else:
                        tile = jnp.full((GRID_W, LANES), NEG_INF, F32)
                    tab_ref[case, a * GRID_W:(a + 1) * GRID_W, t * LANES:(t + 1) * LANES] = tile

    def step(gp, carry):
        groups = [NA_GROUPS_PER_STEP * gp + a for a in range(NA_GROUPS_PER_STEP)]
        qs = [pl.multiple_of(gi * NA_Q, NA_Q) for gi in groups]
        ks = [pl.multiple_of(jnp.clip(gi * NA_Q_ROWS - NA_KH // 2, 0, rows - NA_K_ROWS) * GRID_W, GRID_W)
              for gi in groups]
        cases = [jnp.where(gi == 0, 0, jnp.where(gi == n_groups - 1, 2, 1)) for gi in groups]
        scores = [_nt_dot(q_ref[pl.ds(qs[a], NA_Q), :], k_ref[pl.ds(ks[a], NA_K), :]) + tab_ref[cases[a]]
                  for a in range(NA_GROUPS_PER_STEP)]
        ps = [jnp.exp2(s - jnp.max(s, axis=-1, keepdims=True)) for s in scores]
        ls = [jnp.sum(p, axis=-1, keepdims=True) for p in ps]
        for a in range(NA_GROUPS_PER_STEP):
            o = jnp.dot(ps[a].astype(BF16), v_ref[pl.ds(ks[a], NA_K), :], preferred_element_type=F32) / ls[a]
            o_ref[pl.ds(qs[a], NA_Q), :] = o.astype(o_ref.dtype)
        return carry

    lax.fori_loop(0, n_groups // NA_GROUPS_PER_STEP, step, 0)


def neighbourhood_attention(proj, rpb, B, S):
    q0 = 3 * A_W // HEAD_DIM
    kernel = functools.partial(_nbr_attn_kernel, S=S)
    return pl.pallas_call(
        kernel,
        grid=(B_HEADS, B),
        in_specs=[
            pl.BlockSpec(memory_space=pltpu.SMEM),
            pl.BlockSpec((None, S, HEAD_DIM), lambda h, b: (b, 0, q0 + h)),
            pl.BlockSpec((None, S, HEAD_DIM), lambda h, b: (b, 0, q0 + B_HEADS + h)),
            pl.BlockSpec((None, S, HEAD_DIM), lambda h, b: (b, 0, q0 + 2 * B_HEADS + h)),
        ],
        out_specs=pl.BlockSpec((None, S, HEAD_DIM), lambda h, b: (b, 0, h)),
        out_shape=jax.ShapeDtypeStruct((B, S, B_W), BF16),
        scratch_shapes=[pltpu.VMEM((RPB_ROWS + 1, GRID_W, LANES), F32),
                        pltpu.VMEM((3, NA_Q, NA_K), F32)],
        compiler_params=_params("arbitrary", "arbitrary"),
        name="neighbourhood_attention",
    )(rpb.astype(F32).reshape(-1), proj, proj, proj)


def _dil_geometry(S, tl, dil):
    half = DIL_RADIUS * dil
    hp = -(-half // LANES) * LANES
    kw = min(S, tl + 2 * hp)
    deltas = [min(max(l0 - hp, 0), S - kw) - l0 for l0 in range(0, S, tl)]
    u0 = -min(deltas)
    return half, hp, kw, u0, kw + max(deltas) + u0


DIL_LATTICE_GROUPS = (1, 2)


def _dilated_kernel(slopes_ref, q0_ref, q1_ref, q2_ref, k0_ref, k1_ref, k2_ref, v0_ref, v1_ref, v2_ref,
                    o_ref, *scratch, S, tl, n_sub, geoms):
    hh = pl.program_id(0)
    b = pl.program_id(1)
    i = pl.program_id(2)
    q_refs, k_refs, v_refs = (q0_ref, q1_ref, q2_ref), (k0_ref, k1_ref, k2_ref), (v0_ref, v1_ref, v2_ref)
    ng = len(C_PATTERNS)
    masked = [g for g in range(ng) if g not in DIL_LATTICE_GROUPS]
    bias_refs = dict(zip(masked, scratch[:len(masked)]))
    wide_ref, qf_ref = scratch[len(masked)], scratch[len(masked) + 1]
    rest = scratch[len(masked) + 2:]
    lattice = {g: rest[4 * n:4 * n + 4] for n, g in enumerate(DIL_LATTICE_GROUPS)}
    rows_step = tl * n_sub
    conv_chunk = 512

    @pl.when(i == 0)
    def _():
        for g in DIL_LATTICE_GROUPS:
            dil = C_PATTERNS[g][1]
            L = S // dil
            for src, dst in ((k_refs[g], lattice[g][0]), (v_refs[g], lattice[g][1])):
                def widen(c, carry, src=src):
                    r0 = pl.multiple_of(c * conv_chunk, conv_chunk)
                    wide_ref[pl.ds(r0, conv_chunk), :] = src[pl.ds(r0, conv_chunk), :].astype(F32)
                    return carry

                lax.fori_loop(0, S // conv_chunk, widen, 0)
                for r in range(dil):
                    dst[r * L:(r + 1) * L, :] = wide_ref[pl.ds(r, L, stride=dil), :].astype(BF16)

    @pl.when(jnp.logical_and(b == 0, i == 0))
    def _():
        for g in masked:
            dil = C_PATTERNS[g][1]
            half, _, _, u0, width = geoms[g]
            slope = slopes_ref[g * C_GROUP_HEADS + hh] * LOG2E
            row = lax.broadcasted_iota(jnp.int32, (tl, LANES), 0)
            col = lax.broadcasted_iota(jnp.int32, (tl, LANES), 1)
            base = col - row - u0

            def body(c, carry, g=g, dil=dil, half=half, slope=slope, base=base):
                off = pl.multiple_of(c * LANES, LANES)
                rel = base + off
                dist = jnp.abs(rel)
                ok = jnp.logical_and(dist <= half, (rel & (dil - 1)) == 0)
                bias_refs[g][:, pl.ds(off, LANES)] = jnp.where(ok, -(slope * dist.astype(F32)), NEG_INF)
                return carry

            lax.fori_loop(0, width // LANES, body, 0)

    gathered = {}
    for n, g in enumerate(DIL_LATTICE_GROUPS):
        dil = C_PATTERNS[g][1]
        nq = rows_step // dil
        qf_ref[n] = q_refs[g][...].astype(F32)
        gathered[g] = [qf_ref[n, pl.ds(r, nq, stride=dil), :].astype(BF16) for r in range(dil)]
    results = {}
    for g in DIL_LATTICE_GROUPS:
        dil = C_PATTERNS[g][1]
        L = S // dil
        nq = rows_step // dil
        kw = min(L, nq + 2 * DIL_RADIUS)
        kd_ref, vd_ref = lattice[g][:2]
        start = pl.multiple_of(jnp.clip(i * nq - DIL_RADIUS, 0, L - kw), DIL_RADIUS)
        mq = i * nq + lax.broadcasted_iota(jnp.int32, (nq, kw), 0)
        mk = start + lax.broadcasted_iota(jnp.int32, (nq, kw), 1)
        steps = jnp.abs(mk - mq)
        slope = slopes_ref[g * C_GROUP_HEADS + hh] * (LOG2E * dil)
        bias = jnp.where(steps <= DIL_RADIUS, -(slope * steps.astype(F32)), NEG_INF)
        ks = [pl.multiple_of(r * L + start, DIL_RADIUS) for r in range(dil)]
        scores = [_nt_dot(gathered[g][r], kd_ref[pl.ds(ks[r], kw), :]) + bias for r in range(dil)]
        ms = [jnp.max(s, axis=-1, keepdims=True) for s in scores]
        ps = [jnp.exp2(s - m) for s, m in zip(scores, ms)]
        ls = [jnp.sum(p, axis=-1, keepdims=True) for p in ps]
        accs = [jnp.dot(ps[r].astype(BF16), vd_ref[pl.ds(ks[r], kw), :], preferred_element_type=F32)
                for r in range(dil)]
        results[g] = [(acc / l, m + jnp.log2(l)) for acc, m, l in zip(accs, ms, ls)]
    for g in DIL_LATTICE_GROUPS:
        dil = C_PATTERNS[g][1]
        nq = rows_step // dil
        out_ref, lse_ref = lattice[g][2:]
        for r, (out, lse) in enumerate(results[g]):
            out_ref[pl.ds(r, nq, stride=dil), :] = out
            lse_ref[pl.ds(r, nq, stride=dil), :] = jnp.broadcast_to(lse, (nq, HEAD_DIM))

    tiles = [slice(r * tl, (r + 1) * tl) for r in range(n_sub)]
    masked_out = {}
    for g in masked:
        _, hp, kw, u0, _ = geoms[g]
        ws = [pl.multiple_of(jnp.clip((i * n_sub + r) * tl - hp, 0, S - kw), LANES) for r in range(n_sub)]
        starts = [pl.multiple_of(ws[r] - (i * n_sub + r) * tl + u0, LANES) for r in range(n_sub)]
        scores = [_nt_dot(q_refs[g][tiles[r], :], k_refs[g][pl.ds(ws[r], kw), :])
                  + bias_refs[g][:, pl.ds(starts[r], kw)] for r in range(n_sub)]
        ms = [jnp.max(sc, axis=-1, keepdims=True) for sc in scores]
        ps = [jnp.exp2(sc - m) for sc, m in zip(scores, ms)]
        ls = [jnp.sum(p, axis=-1, keepdims=True) for p in ps]
        accs = [jnp.dot(ps[r].astype(BF16), v_refs[g][pl.ds(ws[r], kw), :], preferred_element_type=F32)
                for r in range(n_sub)]
        masked_out[g] = [(acc / l, m + jnp.log2(l)) for acc, m, l in zip(accs, ms, ls)]

    for r in range(n_sub):
        outs, lses = [], []
        for g in range(ng):
            if g in lattice:
                outs.append(lattice[g][2][tiles[r], :])
                lses.append(lattice[g][3][tiles[r], :])
            else:
                outs.append(masked_out[g][r][0])
                lses.append(masked_out[g][r][1])
        top = jnp.maximum(jnp.maximum(lses[0], lses[1]), lses[2])
        ws_ = [jnp.exp2(lse - top) for lse in lses]
        num = ws_[0] * outs[0] + ws_[1] * outs[1] + ws_[2] * outs[2]
        o_ref[tiles[r], :] = (num / (ws_[0] + ws_[1] + ws_[2])).astype(o_ref.dtype)


def dilated_mixture_attention(proj, B, S):
    tl = ATTN_TQ
    ng = len(C_PATTERNS)
    for win, dil in C_PATTERNS:
        assert win // (2 * dil) == DIL_RADIUS and dil & (dil - 1) == 0
    geoms = tuple(_dil_geometry(S, tl, dil) for _, dil in C_PATTERNS)
    slopes = jnp.asarray([2.0 ** (-8.0 * (i + 1) / C_HEADS) for i in range(C_HEADS)], F32)
    k0, v0 = C_W // HEAD_DIM, 2 * C_W // HEAD_DIM

    n_sub = DENSE_Q_TILES
    rows_step = tl * n_sub

    def q_spec(g):
        return pl.BlockSpec((None, rows_step, HEAD_DIM), lambda hh, b, i: (b, i, g * C_GROUP_HEADS + hh))

    def kv_spec(first, g):
        return pl.BlockSpec((None, S, HEAD_DIM), lambda hh, b, i: (b, 0, first + g * C_GROUP_HEADS + hh))

    masked = [g for g in range(ng) if g not in DIL_LATTICE_GROUPS]
    scratch = [pltpu.VMEM((tl, geoms[g][4]), F32) for g in masked]
    scratch += [pltpu.VMEM((S, HEAD_DIM), F32), pltpu.VMEM((len(DIL_LATTICE_GROUPS), rows_step, HEAD_DIM), F32)]
    for g in DIL_LATTICE_GROUPS:
        assert rows_step % (C_PATTERNS[g][1] * 8) == 0
        scratch += [pltpu.VMEM((S, HEAD_DIM), BF16)] * 2 + [pltpu.VMEM((rows_step, HEAD_DIM), F32)] * 2

    kernel = functools.partial(_dilated_kernel, S=S, tl=tl, n_sub=n_sub, geoms=geoms)
    return pl.pallas_call(
        kernel,
        grid=(C_GROUP_HEADS, B, S // rows_step),
        in_specs=([pl.BlockSpec(memory_space=pltpu.SMEM)]
                  + [q_spec(g) for g in range(ng)]
                  + [kv_spec(k0, g) for g in range(ng)]
                  + [kv_spec(v0, g) for g in range(ng)]),
        out_specs=pl.BlockSpec((None, rows_step, HEAD_DIM), lambda hh, b, i: (b, i, hh)),
        out_shape=jax.ShapeDtypeStruct((B, S, OC_W), BF16),
        scratch_shapes=scratch,
        compiler_params=_params("arbitrary", "arbitrary", "arbitrary"),
        name="dilated_mixture_attention",
    )(slopes, *([proj] * (3 * ng)))


def _gqa_kernel(q_ref, k_ref, v_ref, o_ref, va_ref, *, S, tq, n_sub, tk):
    d = HEAD_DIM

    @pl.when(pl.program_id(2) == 0)
    def _():
        _fill_value_and_ones(v_ref, va_ref)

    chains = [(r, g) for r in range(n_sub) for g in range(D_GROUP)]
    acc, _ = _online_softmax(
        len(chains),
        lambda n, j: q_ref[chains[n][0] * tq:(chains[n][0] + 1) * tq, chains[n][1] * d:(chains[n][1] + 1) * d],
        lambda n, j: k_ref[j * tk:(j + 1) * tk, :],
        lambda n, j: va_ref[j * tk:(j + 1) * tk, :],
        lambda n, j: None,
        S // tk, row_sums=False)
    for n, (r, g) in enumerate(chains):
        o_ref[r * tq:(r + 1) * tq, g * d:(g + 1) * d] = (acc[n][:, :d] / acc[n][:, d:]).astype(o_ref.dtype)


def gqa_attention(proj, B, S):
    tq = ATTN_TQ
    qw = D_GROUP * HEAD_DIM
    q0 = 3 * C_W // qw
    k0 = (3 * C_W + DQ_W) // HEAD_DIM
    v0 = k0 + D_KV_HEADS
    n_sub = DENSE_Q_TILES
    tk = min(GQA_TK, S)
    return pl.pallas_call(
        functools.partial(_gqa_kernel, S=S, tq=tq, n_sub=n_sub, tk=tk),
        grid=(D_KV_HEADS, B, S // (tq * n_sub)),
        in_specs=[
            pl.BlockSpec((None, tq * n_sub, qw), lambda h, b, i: (b, i, q0 + h)),
            pl.BlockSpec((None, S, HEAD_DIM), lambda h, b, i: (b, 0, k0 + h)),
            pl.BlockSpec((None, S, HEAD_DIM), lambda h, b, i: (b, 0, v0 + h)),
        ],
        out_specs=pl.BlockSpec((None, tq * n_sub, qw), lambda h, b, i: (b, i, h)),
        out_shape=jax.ShapeDtypeStruct((B, S, DQ_W), BF16),
        scratch_shapes=[pltpu.VMEM((S, 2 * HEAD_DIM), BF16)],
        compiler_params=_params("arbitrary", "arbitrary", "arbitrary"),
        name="gqa_attention",
    )(proj, proj, proj)


def _out_proj_kernel(a1_ref, a2_ref, w1_ref, w2_ref, h_ref, o_ref):
    acc = jnp.dot(a1_ref[...], w1_ref[...], preferred_element_type=F32)
    acc = acc + jnp.dot(a2_ref[...], w2_ref[...], preferred_element_type=F32)
    o_ref[...] = h_ref[...] + acc


def out_proj_residual(a1, a2, w1, w2, h):
    T, D = h.shape
    k1, k2 = a1.shape[1], a2.shape[1]
    tm = OUT_PROJ_TM
    resident = pl.Buffered(1)
    return pl.pallas_call(
        _out_proj_kernel,
        grid=(T // tm,),
        in_specs=[
            pl.BlockSpec((tm, k1), lambda i: (i, 0)),
            pl.BlockSpec((tm, k2), lambda i: (i, 0)),
            pl.BlockSpec((k1, D), lambda i: (0, 0), pipeline_mode=resident),
            pl.BlockSpec((k2, D), lambda i: (0, 0), pipeline_mode=resident),
            pl.BlockSpec((tm, D), lambda i: (i, 0)),
        ],
        out_specs=pl.BlockSpec((tm, D), lambda i: (i, 0)),
        out_shape=jax.ShapeDtypeStruct((T, D), F32),
        compiler_params=_params("parallel"),
        name="out_proj_residual",
    )(a1, a2, w1, w2, h)


def _ffn_kernel(x_ref, g_ref, wg_ref, wu_ref, wd_ref, fg_ref, o_ref, xn_ref, *, final_norm):
    k = pl.program_id(1)

    @pl.when(k == 0)
    def _():
        _norm_rows_to(x_ref, g_ref, xn_ref, x_ref.shape[0])
        o_ref[...] = x_ref[...]

    xn = xn_ref[...]
    gate = jnp.dot(xn, wg_ref[...], preferred_element_type=F32)
    up = jnp.dot(xn, wu_ref[...], preferred_element_type=F32)
    act = (gate * jax.nn.sigmoid(gate)) * up
    o_ref[...] += jnp.dot(act.astype(BF16), wd_ref[...], preferred_element_type=F32)

    if final_norm:
        @pl.when(k == pl.num_programs(1) - 1)
        def _():
            fg = fg_ref[...]

            def body(c, carry):
                r = pl.multiple_of(c * NORM_ROW_CHUNK, NORM_ROW_CHUNK)
                o_ref[pl.ds(r, NORM_ROW_CHUNK), :] = _rms(o_ref[pl.ds(r, NORM_ROW_CHUNK), :], fg)
                return carry

            lax.fori_loop(0, o_ref.shape[0] // NORM_ROW_CHUNK, body, 0)


def ffn_residual(x, g, wg, wu, wd, final_g, final_norm):
    T, D = x.shape
    H = wg.shape[1]
    tm, th = FFN_TM, FFN_TH
    kernel = functools.partial(_ffn_kernel, final_norm=final_norm)
    return pl.pallas_call(
        kernel,
        grid=(T // tm, H // th),
        in_specs=[
            pl.BlockSpec((tm, D), lambda i, k: (i, 0)),
            pl.BlockSpec((1, D), lambda i, k: (0, 0)),
            pl.BlockSpec((D, th), lambda i, k: (0, k)),
            pl.BlockSpec((D, th), lambda i, k: (0, k)),
            pl.BlockSpec((th, D), lambda i, k: (k, 0)),
            pl.BlockSpec((1, D), lambda i, k: (0, 0)),
        ],
        out_specs=pl.BlockSpec((tm, D), lambda i, k: (i, 0)),
        out_shape=jax.ShapeDtypeStruct((T, D), F32),
        scratch_shapes=[pltpu.VMEM((tm, D), BF16)],
        compiler_params=_params("parallel", "arbitrary"),
        name="ffn_residual",
    )(x, g.reshape(1, D), wg, wu, wd, final_g.reshape(1, D))


def _rope_tables(S):
    t = jnp.arange(S)
    row = (t // GRID_W).astype(F32)
    col = (t % GRID_W).astype(F32)
    half = HEAD_DIM // 2
    f_row = ROPE_THETA ** (-jnp.arange(0, half, 2, dtype=F32) / half)
    f_col = ROPE_THETA ** (-jnp.arange(0, HEAD_DIM - half, 2, dtype=F32) / (HEAD_DIM - half))
    ang = jnp.concatenate([row[:, None] * f_row[None, :], col[:, None] * f_col[None, :]], axis=-1)
    cos = jnp.repeat(jnp.cos(ang), 2, axis=-1)
    sin = jnp.repeat(jnp.sin(ang), 2, axis=-1)
    sign = jnp.where(jnp.arange(HEAD_DIM) % 2 == 0, -1.0, 1.0).astype(F32)
    return cos, sin * sign[None, :]


def _lambda_init(layer_idx):
    return 0.8 - 0.6 * math.exp(-0.3 * layer_idx)


def _trunk(x, w):
    B, S, D = x.shape
    T = B * S
    h = x.reshape(T, D)
    cos, sin_signed = _rope_tables(S)
    for i in range(DEPTH):
        j = i // 2
        if i % 2 == 0:
            proj = norm_proj(h, w["attn_norm_g"][i], w["ev_w_in"][j])
            proj3 = proj.reshape(B, S, EV_IN)
            oa = diff_attention(proj3, w["ev_lambda"][j], w["ev_subln_g"][j], B, S, _lambda_init(i))
            ob = neighbourhood_attention(proj3, w["ev_rpb"][j], B, S)
            h = out_proj_residual(oa.reshape(T, A_W), ob.reshape(T, B_W),
                                  w["ev_w_out"][j][0], w["ev_w_out"][j][1], h)
        else:
            proj = od_norm_proj(h, w["attn_norm_g"][i], w["od_w_in"][j], cos, sin_signed,
                                w["od_qk_norm_g"][j].astype(F32), S)
            proj3 = proj.reshape(B, S, OD_IN)
            oc = dilated_mixture_attention(proj3, B, S).reshape(T, OC_W)
            od = gqa_attention(proj3, B, S)
            h = out_proj_residual(oc, od.reshape(T, DQ_W), w["od_w_out"][j][0], w["od_w_out"][j][1], h)
        h = ffn_residual(h, w["ffn_norm_g"][i], w["ffn_w_gate"][i], w["ffn_w_up"][i], w["ffn_w_down"][i],
                         w["final_norm_g"], final_norm=(i == DEPTH - 1))
    return h.reshape(B, S, D)


def kernel(x_prompt, x_sample, attn_norm_g, ev_w_in, ev_lambda, ev_subln_g, ev_rpb, ev_w_out, od_w_in,
           od_qk_norm_g, od_w_out, ffn_norm_g, ffn_w_gate, ffn_w_up, ffn_w_down, final_norm_g):
    w = {
        "attn_norm_g": attn_norm_g, "ffn_norm_g": ffn_norm_g, "final_norm_g": final_norm_g,
        "ev_lambda": ev_lambda, "ev_subln_g": ev_subln_g, "ev_rpb": ev_rpb, "od_qk_norm_g": od_qk_norm_g,
        "ev_w_in": [ev_w_in[j].astype(BF16) for j in range(ev_w_in.shape[0])],
        "ev_w_out": [(ev_w_out[j, :A_W].astype(BF16), ev_w_out[j, A_W:].astype(BF16))
                     for j in range(ev_w_out.shape[0])],
        "od_w_in": [od_w_in[j].astype(BF16) for j in range(od_w_in.shape[0])],
        "od_w_out": [(od_w_out[j, :OC_W].astype(BF16), od_w_out[j, OC_W:].astype(BF16))
                     for j in range(od_w_out.shape[0])],
        "ffn_w_gate": [ffn_w_gate[i].astype(BF16) for i in range(DEPTH)],
        "ffn_w_up": [ffn_w_up[i].astype(BF16) for i in range(DEPTH)],
        "ffn_w_down": [ffn_w_down[i].astype(BF16) for i in range(DEPTH)],
    }
    return _trunk(x_prompt, w), _trunk(x_sample, w)
```
